```python
import math
import jax, jax.numpy as jnp
from jax import lax
import numpy as np

D_MODEL = 2048
BATCH = 8
SEQ = 4096
DEPTH = 2

N_META = 16
D_MIX = D_MODEL
CONV_W = D_MIX // 4
CONV_K = 31
HEAD_DIM = 64
N_HEADS = (D_MIX // 2) // HEAD_DIM
N_KV = 4
GROUP = N_HEADS // N_KV
ATT_W = N_HEADS * HEAD_DIM
KV_W = N_KV * HEAD_DIM
WINDOW = 128
BLOCK = 128
ROT_DIM = HEAD_DIM // 4
ROPE_THETA = 500000.0
LRU_W = D_MIX // 4
LRU_HEADS = 8
LRU_HEAD_DIM = LRU_W // LRU_HEADS
LRU_CONV_K = 4
LRU_C = 8.0
IN_WIDTHS = (CONV_W, CONV_W, CONV_W,
             ATT_W, KV_W, KV_W, ATT_W,
             LRU_W, LRU_W)
IN_TOTAL = sum(IN_WIDTHS)
OUT_IN = CONV_W + ATT_W + LRU_W
LN_EPS = 1e-5
DEEPNORM_ALPHA = (2.0 * DEPTH) ** 0.25
DEEPNORM_BETA = (8.0 * DEPTH) ** -0.25
NEG_INF = -1e30

kernel_name = "hymba_style_conv_swa_rglru_deepnorm"


def _layer_norm(x, g, b):
    xf = x.astype(jnp.float32)
    mu = jnp.mean(xf, axis=-1, keepdims=True)
    var = jnp.mean(jnp.square(xf - mu), axis=-1, keepdims=True)
    y = (xf - mu) * lax.rsqrt(var + LN_EPS)
    return (y * g.astype(jnp.float32) + b.astype(jnp.float32)).astype(x.dtype)


def _causal_depthwise_conv(x, w, b):
    k = w.shape[0]
    y = lax.conv_general_dilated(
        x, w[:, None, :].astype(x.dtype), window_strides=(1,), padding=[(k - 1, 0)],
        dimension_numbers=("NWC", "WIO", "NWC"), feature_group_count=x.shape[-1])
    return y + b


def _partial_rotary(x, pos):
    half = ROT_DIM // 2
    inv_freq = ROPE_THETA ** (-jnp.arange(half, dtype=jnp.float32) / half)
    ang = pos.astype(jnp.float32)[:, None] * inv_freq[None, :]
    cos = jnp.cos(ang)[None, :, None, :]
    sin = jnp.sin(ang)[None, :, None, :]
    x1 = x[..., :half].astype(jnp.float32)
    x2 = x[..., half:ROT_DIM].astype(jnp.float32)
    rot = jnp.concatenate([x1 * cos - x2 * sin, x2 * cos + x1 * sin], axis=-1).astype(x.dtype)
    return jnp.concatenate([rot, x[..., ROT_DIM:]], axis=-1)


def _sliding_window_sink_attention(q, k, v, sinks):
    B, L, H, Dh = q.shape
    S = L - N_META
    nblk = S // BLOCK
    scale = Dh ** -0.5
    sink_kg = sinks.astype(jnp.float32).reshape(N_KV, GROUP)

    qm, qr = q[:, :N_META], q[:, N_META:]
    km, kr = k[:, :N_META], k[:, N_META:]
    vm, vr = v[:, :N_META], v[:, N_META:]

    qb = qr.reshape(B, nblk, BLOCK, N_KV, GROUP, Dh)
    kb = kr.reshape(B, nblk, BLOCK, N_KV, Dh)
    vb = vr.reshape(B, nblk, BLOCK, N_KV, Dh)
    pad = ((0, 0), (1, 0), (0, 0), (0, 0), (0, 0))
    k_band = jnp.concatenate([jnp.pad(kb, pad)[:, :-1], kb], axis=2)
    v_band = jnp.concatenate([jnp.pad(vb, pad)[:, :-1], vb], axis=2)

    s_meta = jnp.einsum("bnqkgd,bmkd->bkgnqm", qb, km).astype(jnp.float32) * scale
    s_band = jnp.einsum("bnqkgd,bnjkd->bkgnqj", qb, k_band).astype(jnp.float32) * scale
    qi = jnp.arange(BLOCK)[:, None]
    kj = jnp.arange(2 * BLOCK)[None, :]
    diff = BLOCK + qi - kj
    in_window = (diff >= 0) & (diff < WINDOW)
    blk = jnp.arange(nblk)[:, None, None]
    valid = in_window[None] & ((kj[None] >= BLOCK) | (blk > 0))
    s_band = jnp.where(valid[None, None, None], s_band, NEG_INF)
    sink_col = jnp.broadcast_to(sink_kg[None, :, :, None, None, None], s_meta.shape[:-1] + (1,))
    probs = jax.nn.softmax(jnp.concatenate([s_meta, s_band, sink_col], axis=-1), axis=-1)
    p_meta = probs[..., :N_META].astype(v.dtype)
    p_band = probs[..., N_META:N_META + 2 * BLOCK].astype(v.dtype)
    out_r = (jnp.einsum("bkgnqm,bmkd->bnqkgd", p_meta, vm)
             + jnp.einsum("bkgnqj,bnjkd->bnqkgd", p_band, v_band)).reshape(B, S, H, Dh)

    qmg = qm.reshape(B, N_META, N_KV, GROUP, Dh)
    s_mm = jnp.einsum("bqkgd,bmkd->bkgqm", qmg, km).astype(jnp.float32) * scale
    causal = jnp.tril(jnp.ones((N_META, N_META), dtype=bool))
    s_mm = jnp.where(causal[None, None, None], s_mm, NEG_INF)
    sink_mm = jnp.broadcast_to(sink_kg[None, :, :, None, None], s_mm.shape[:-1] + (1,))
    p_mm = jax.nn.softmax(jnp.concatenate([s_mm, sink_mm], axis=-1), axis=-1)[..., :N_META]
    out_m = jnp.einsum("bkgqm,bmkd->bqkgd", p_mm.astype(v.dtype), vm).reshape(B, N_META, H, Dh)

    return jnp.concatenate([out_m, out_r], axis=1)


def _rg_lru(x, w_a, b_a, w_x, b_x, lam):
    B, L, C = x.shape
    xh = x.reshape(B, L, LRU_HEADS, LRU_HEAD_DIM)
    r = jax.nn.sigmoid(jnp.einsum("blhi,hij->blhj", xh, w_a).reshape(B, L, C) + b_a)
    i = jax.nn.sigmoid(jnp.einsum("blhi,hij->blhj", xh, w_x).reshape(B, L, C) + b_x)
    log_a = -LRU_C * r.astype(jnp.float32) * jax.nn.softplus(-lam.astype(jnp.float32))
    a = jnp.exp(log_a)
    u = jnp.sqrt(-jnp.expm1(2.0 * log_a)) * (i * x).astype(jnp.float32)

    def combine(c1, c2):
        a1, b1 = c1
        a2, b2 = c2
        return a1 * a2, a2 * b1 + b2

    _, h = lax.associative_scan(combine, (a, u), axis=1)
    return h.astype(x.dtype)


def _hybrid_layer(h, pos, w_in, conv_dw_w, conv_dw_b, conv_ln_g, conv_ln_b, conv_pw_w, conv_pw_b,
                  attn_sinks, lru_conv_w, lru_conv_b, lru_wa, lru_ba, lru_wx, lru_bx, lru_lambda,
                  w_out, ln_post_g, ln_post_b):
    B, L, _ = h.shape
    proj = h @ w_in
    split_pts = [int(s) for s in np.cumsum(IN_WIDTHS)[:-1]]
    (c_val, c_glu, c_gate, q, k, v, a_gate, r_x, r_gate) = jnp.split(proj, split_pts, axis=-1)

    c = c_val * jax.nn.sigmoid(c_glu)
    c = _causal_depthwise_conv(c, conv_dw_w, conv_dw_b)
    c = jax.nn.silu(_layer_norm(c, conv_ln_g, conv_ln_b))
    c = c @ conv_pw_w + conv_pw_b
    y_conv = c * jax.nn.silu(c_gate)

    q = _partial_rotary(q.reshape(B, L, N_HEADS, HEAD_DIM), pos)
    k = _partial_rotary(k.reshape(B, L, N_KV, HEAD_DIM), pos)
    v = v.reshape(B, L, N_KV, HEAD_DIM)
    att = _sliding_window_sink_attention(q, k, v, attn_sinks).reshape(B, L, ATT_W)
    y_attn = att * jax.nn.silu(a_gate)

    r = _causal_depthwise_conv(r_x, lru_conv_w, lru_conv_b)
    r = _rg_lru(r, lru_wa, lru_ba, lru_wx, lru_bx, lru_lambda)
    y_lru = r * jax.nn.silu(r_gate)

    mixed = jnp.concatenate([y_conv, y_attn, y_lru], axis=-1) @ w_out
    return _layer_norm(DEEPNORM_ALPHA * h + mixed, ln_post_g, ln_post_b)


def setup_inputs(seed: int = 0) -> dict:
    key = jax.random.key(seed)
    ks = jax.random.split(key, 24)
    f32 = jnp.float32
    nrm = lambda k, shape, s: jax.random.normal(k, shape, f32) * s
    u = jax.random.uniform(ks[17], (DEPTH, LRU_W), f32, 0.9, 0.999)
    s = u ** (1.0 / LRU_C)
    lam = jnp.log(s) - jnp.log1p(-s)
    return {
        "x": nrm(ks[0], (BATCH, SEQ, D_MODEL), 1.0),
        "meta_tokens": nrm(ks[1], (N_META, D_MODEL), 1.0),
        "ln_in_g": 1.0 + nrm(ks[2], (D_MODEL,), 0.02),
        "ln_in_b": nrm(ks[3], (D_MODEL,), 0.02),
        "w_in": nrm(ks[4], (DEPTH, D_MODEL, IN_TOTAL), D_MODEL ** -0.5),
        "conv_dw_w": nrm(ks[5], (DEPTH, CONV_K, CONV_W), CONV_K ** -0.5),
        "conv_dw_b": nrm(ks[6], (DEPTH, CONV_W), 0.01),
        "conv_ln_g": 1.0 + nrm(ks[7], (DEPTH, CONV_W), 0.02),
        "conv_ln_b": nrm(ks[8], (DEPTH, CONV_W), 0.02),
        "conv_pw_w": nrm(ks[9], (DEPTH, CONV_W, CONV_W), DEEPNORM_BETA * CONV_W ** -0.5),
        "conv_pw_b": nrm(ks[10], (DEPTH, CONV_W), 0.01),
        "attn_sinks": nrm(ks[11], (DEPTH, N_HEADS), 0.5),
        "lru_conv_w": nrm(ks[12], (DEPTH, LRU_CONV_K, LRU_W), LRU_CONV_K ** -0.5),
        "lru_conv_b": nrm(ks[13], (DEPTH, LRU_W), 0.01),
        "lru_wa": nrm(ks[14], (DEPTH, LRU_HEADS, LRU_HEAD_DIM, LRU_HEAD_DIM), LRU_HEAD_DIM ** -0.5),
        "lru_ba": nrm(ks[15], (DEPTH, LRU_W), 0.01),
        "lru_wx": nrm(ks[16], (DEPTH, LRU_HEADS, LRU_HEAD_DIM, LRU_HEAD_DIM), LRU_HEAD_DIM ** -0.5),
        "lru_bx": nrm(ks[18], (DEPTH, LRU_W), 0.01),
        "lru_lambda": lam,
        "w_out": nrm(ks[19], (DEPTH, OUT_IN, D_MODEL), DEEPNORM_BETA * OUT_IN ** -0.5),
        "ln_post_g": 1.0 + nrm(ks[20], (DEPTH, D_MODEL), 0.02),
        "ln_post_b": nrm(ks[21], (DEPTH, D_MODEL), 0.02),
    }


def reference(x, meta_tokens, ln_in_g, ln_in_b, w_in, conv_dw_w, conv_dw_b, conv_ln_g, conv_ln_b,
              conv_pw_w, conv_pw_b, attn_sinks, lru_conv_w, lru_conv_b, lru_wa, lru_ba, lru_wx,
              lru_bx, lru_lambda, w_out, ln_post_g, ln_post_b):
    B = x.shape[0]
    meta = jnp.broadcast_to(meta_tokens[None].astype(x.dtype), (B, N_META, x.shape[-1]))
    h = jnp.concatenate([meta, x], axis=1)
    h = _layer_norm(h, ln_in_g, ln_in_b)
    pos = jnp.arange(h.shape[1], dtype=jnp.int32)
    for l in range(DEPTH):
        h = _hybrid_layer(h, pos, w_in[l], conv_dw_w[l], conv_dw_b[l], conv_ln_g[l], conv_ln_b[l],
                          conv_pw_w[l], conv_pw_b[l], attn_sinks[l], lru_conv_w[l], lru_conv_b[l],
                          lru_wa[l], lru_ba[l], lru_wx[l], lru_bx[l], lru_lambda[l],
                          w_out[l], ln_post_g[l], ln_post_b[l])
    return h[:, N_META:]
```

```python
import functools
import math

import jax
import jax.numpy as jnp
import numpy as np
from jax import lax
from jax.experimental import pallas as pl
from jax.experimental.pallas import tpu as pltpu

F32 = jnp.float32
BF16 = jnp.bfloat16

D_MODEL = 2048
N_META = 16
CONV_W = 512
CONV_K = 31
HEAD_DIM = 64
N_HEADS = 16
N_KV = 4
GROUP = N_HEADS // N_KV
ATT_W = N_HEADS * HEAD_DIM
KV_W = N_KV * HEAD_DIM
WINDOW = 128
ROT_DIM = HEAD_DIM // 4
ROPE_THETA = 500000.0
LRU_W = 512
LRU_HEADS = 8
LRU_CONV_K = 4
LRU_C = 8.0
IN_TOTAL = 3 * CONV_W + 2 * ATT_W + 2 * KV_W + 2 * LRU_W
COL_ATT = 3 * CONV_W
COL_LRU = COL_ATT + 2 * ATT_W + 2 * KV_W
OUT_IN = CONV_W + ATT_W + LRU_W
LN_EPS = 1e-5
DEPTH = 2
DEEPNORM_ALPHA = (2.0 * DEPTH) ** 0.25
NEG_INF = -1e30
ATT_SCALE = HEAD_DIM ** -0.5

LANES = 128
SUBLANES = 8
T_CHUNK = 256
CONV_HALO = 32
LRU_HALO = SUBLANES
HALF = WINDOW // 2
KEYS = WINDOW + HALF
KPAD = 256
VMEM_LIMIT_MAIN = 58 * 1024 * 1024
VMEM_LIMIT_META = 48 * 1024 * 1024


def _silu(x):
    return x * jax.nn.sigmoid(x)


def _layer_norm(x, g, b):
    mu = jnp.mean(x, axis=-1, keepdims=True)
    xc = x - mu
    var = jnp.mean(xc * xc, axis=-1, keepdims=True)
    return xc * lax.rsqrt(var + LN_EPS) * g + b


def _rotary(x, cos, s1, s2):
    return x * cos + pltpu.roll(x, LANES - ROT_DIM // 2, 1) * s1 + pltpu.roll(x, ROT_DIM // 2, 1) * s2


def _conv_branch(rows, rb, pc_ref, cbuf_ref, dw_w_ref, dw_b_ref, g_ref, b_ref, pw_w_ref, pw_b_ref, ycat_ref):
    for r in range(0, rows, rb):
        cbuf_ref[CONV_HALO + r:CONV_HALO + r + rb, :] = (
            pc_ref[r:r + rb, 0:CONV_W] * jax.nn.sigmoid(pc_ref[r:r + rb, CONV_W:2 * CONV_W]))
    ys = []
    for r in range(0, rows, rb):
        acc = jnp.broadcast_to(dw_b_ref[...], (rb, CONV_W))
        for k in range(CONV_K):
            start = CONV_HALO - (CONV_K - 1) + k + r
            acc = acc + dw_w_ref[k:k + 1, :] * cbuf_ref[start:start + rb, :]
        ys.append(_silu(_layer_norm(acc, g_ref[...], b_ref[...])).astype(BF16))
    y = jnp.concatenate(ys, axis=0) if len(ys) > 1 else ys[0]
    z = jnp.dot(y, pw_w_ref[...], preferred_element_type=F32) + pw_b_ref[...]
    for r in range(0, rows, rb):
        ycat_ref[r:r + rb, 0:CONV_W] = (z[r:r + rb] * _silu(pc_ref[r:r + rb, 2 * CONV_W:3 * CONV_W])).astype(BF16)


def _lru_branch(rows, rb, pr_ref, rbuf_ref, xc_ref, gb_ref, hcar_ref, lcw_ref, lcb_ref, wg_ref, bg_ref, lam_ref,
                ycat_ref, hout_ref=None):
    rbuf_ref[LRU_HALO:LRU_HALO + rows, :] = pr_ref[:, 0:LRU_W]
    for r in range(0, rows, rb):
        acc = jnp.broadcast_to(lcb_ref[...], (rb, LRU_W))
        for k in range(LRU_CONV_K):
            start = LRU_HALO - (LRU_CONV_K - 1) + k + r
            acc = acc + lcw_ref[k:k + 1, :] * rbuf_ref[start:start + rb, :]
        xc_ref[r:r + rb, :] = acc
    gb_ref[...] = jnp.dot(xc_ref[...].astype(BF16), wg_ref[...], preferred_element_type=F32) + bg_ref[...]

    neg_c_softplus = -LRU_C * jax.nn.softplus(-lam_ref[...])
    sub = lax.broadcasted_iota(jnp.int32, (SUBLANES, LRU_W), 0)
    carry = hcar_ref[...]
    for r in range(0, rows, SUBLANES):
        x = xc_ref[r:r + SUBLANES, :]
        gate_r = jax.nn.sigmoid(gb_ref[r:r + SUBLANES, 0:LRU_W])
        gate_i = jax.nn.sigmoid(gb_ref[r:r + SUBLANES, LRU_W:2 * LRU_W])
        log_a = gate_r * neg_c_softplus
        a = jnp.exp(log_a)
        u = jnp.sqrt(1.0 - a * a) * (gate_i * x)
        for d in (1, 2, 4):
            keep = sub >= d
            a_sh = jnp.where(keep, pltpu.roll(a, d, 0), 1.0)
            u_sh = jnp.where(keep, pltpu.roll(u, d, 0), 0.0)
            u = a * u_sh + u
            a = a * a_sh
        h = a * carry + u
        carry = jnp.broadcast_to(h[SUBLANES - 1:SUBLANES, :], (SUBLANES, LRU_W))
        if hout_ref is not None:
            hout_ref[r:r + SUBLANES, :] = h
        ycat_ref[r:r + SUBLANES, CONV_W + ATT_W:OUT_IN] = (
            h * _silu(pr_ref[r:r + SUBLANES, LRU_W:2 * LRU_W])).astype(BF16)
    hcar_ref[...] = carry


def _out_proj(rows, rb, resid_ref, ycat_ref, w_out_ref, mixed_ref, g_ref, b_ref, store):
    mixed_ref[...] = jnp.dot(ycat_ref[...], w_out_ref[...], preferred_element_type=F32)
    for r in range(0, rows, rb):
        y = DEEPNORM_ALPHA * resid_ref[r:r + rb, :] + mixed_ref[r:r + rb, :]
        store(r, _layer_norm(y, g_ref[...], b_ref[...]))


def _main_kernel(first_layer,
                 x_ref, cos_ref, s1_ref, s2_ref, lng_ref, lnb_ref, w_in_ref,
                 dw_w_ref, dw_b_ref, cg_ref, cb_ref, pw_w_ref, pw_b_ref, sink_ref,
                 lcw_ref, lcb_ref, wg_ref, bg_ref, lam_ref, w_out_ref, pg_ref, pb_ref,
                 cmeta_ref, kmeta_ref, vmeta_ref, rxmeta_ref, hlmeta_ref,
                 out_ref,
                 hb_ref, hres_ref, pc_ref, pa_ref, pr_ref, gb_ref, mixed_ref, ycat_ref,
                 qbuf_ref, kbuf_ref, vbuf_ref, kmz_ref, vmz_ref, cbuf_ref, rbuf_ref, xc_ref, hcar_ref):
    T = T_CHUNK
    t = pl.program_id(1)

    @pl.when(t == 0)
    def _start_of_sequence():
        cbuf_ref[0:CONV_HALO - N_META, :] = jnp.zeros((CONV_HALO - N_META, CONV_W), F32)
        cbuf_ref[CONV_HALO - N_META:CONV_HALO, :] = cmeta_ref[...]
        rbuf_ref[0:LRU_HALO, :] = rxmeta_ref[N_META - LRU_HALO:N_META, :]
        hcar_ref[...] = jnp.broadcast_to(hlmeta_ref[N_META - 1:N_META, :], (SUBLANES, LRU_W))
        kbuf_ref[:, 0:WINDOW, :] = jnp.zeros((N_KV, WINDOW, HEAD_DIM), BF16)
        vbuf_ref[:, 0:WINDOW, :] = jnp.zeros((N_KV, WINDOW, HEAD_DIM), BF16)
        pad = jnp.zeros((KPAD - KEYS - N_META, HEAD_DIM), BF16)
        for g in range(N_KV):
            kmz_ref[g, 0:N_META, :] = kmeta_ref[:, g * HEAD_DIM:(g + 1) * HEAD_DIM].astype(BF16)
            kmz_ref[g, N_META:KPAD - KEYS, :] = pad
            vmz_ref[g, 0:N_META, :] = vmeta_ref[:, g * HEAD_DIM:(g + 1) * HEAD_DIM].astype(BF16)
            vmz_ref[g, N_META:KPAD - KEYS, :] = pad

    RB = 32
    if first_layer:
        for r in range(0, T, RB):
            h = _layer_norm(x_ref[0, r:r + RB, :], lng_ref[...], lnb_ref[...])
            hres_ref[r:r + RB, :] = h
            hb_ref[r:r + RB, :] = h.astype(BF16)
        resid_ref = hres_ref
    else:
        for r in range(0, T, RB):
            hb_ref[r:r + RB, :] = x_ref[0, r:r + RB, :].astype(BF16)
        resid_ref = x_ref.at[0]

    pc_ref[...] = jnp.dot(hb_ref[...], w_in_ref[:, 0:COL_ATT], preferred_element_type=F32)
    _conv_branch(T, RB, pc_ref, cbuf_ref, dw_w_ref, dw_b_ref, cg_ref, cb_ref, pw_w_ref, pw_b_ref, ycat_ref)
    cbuf_ref[0:CONV_HALO, :] = cbuf_ref[T:T + CONV_HALO, :]

    pa_ref[...] = jnp.dot(hb_ref[...], w_in_ref[:, COL_ATT:COL_LRU], preferred_element_type=F32)
    cos, s1, s2 = cos_ref[...], s1_ref[...], s2_ref[...]
    for j in range(ATT_W // LANES):
        rot = _rotary(pa_ref[:, j * LANES:(j + 1) * LANES], cos, s1, s2) * ATT_SCALE
        qbuf_ref[2 * j, :, :] = rot[:, 0:HEAD_DIM].astype(BF16)
        qbuf_ref[2 * j + 1, :, :] = rot[:, HEAD_DIM:LANES].astype(BF16)
    for j in range(KV_W // LANES):
        rot = _rotary(pa_ref[:, ATT_W + j * LANES:ATT_W + (j + 1) * LANES], cos, s1, s2)
        kbuf_ref[2 * j, WINDOW:WINDOW + T, :] = rot[:, 0:HEAD_DIM].astype(BF16)
        kbuf_ref[2 * j + 1, WINDOW:WINDOW + T, :] = rot[:, HEAD_DIM:LANES].astype(BF16)
        v = pa_ref[:, ATT_W + KV_W + j * LANES:ATT_W + KV_W + (j + 1) * LANES]
        vbuf_ref[2 * j, WINDOW:WINDOW + T, :] = v[:, 0:HEAD_DIM].astype(BF16)
        vbuf_ref[2 * j + 1, WINDOW:WINDOW + T, :] = v[:, HEAD_DIM:LANES].astype(BF16)

    nrow = GROUP * HALF
    qi = lax.broadcasted_iota(jnp.int32, (nrow, KPAD), 0) % HALF
    kc = lax.broadcasted_iota(jnp.int32, (nrow, KPAD), 1)
    visible = ((kc > qi) & (kc <= qi + WINDOW) & (kc < KEYS)) | ((kc >= KEYS) & (kc < KEYS + N_META))
    rowp = lax.broadcasted_iota(jnp.int32, (nrow, 1), 0) // HALF
    a_gate0 = ATT_W + 2 * KV_W
    for g in range(N_KV):
        sink_col = jnp.zeros((nrow, 1), F32)
        for p in range(GROUP):
            sink_col = jnp.where(rowp == p, sink_ref[g * GROUP + p], sink_col)
        for blk in range(T // WINDOW):
            for s in range(2):
                r0 = blk * WINDOW + s * HALF
                q = jnp.concatenate([qbuf_ref[g * GROUP + p, r0:r0 + HALF, :] for p in range(GROUP)], axis=0)
                kw = jnp.concatenate([kbuf_ref[g, r0:r0 + KEYS, :], kmz_ref[g]], axis=0)
                vw = jnp.concatenate([vbuf_ref[g, r0:r0 + KEYS, :], vmz_ref[g]], axis=0)
                sc = lax.dot_general(q, kw, (((1,), (1,)), ((), ())), preferred_element_type=F32)
                if blk == 0:
                    first_key = jnp.where(t == 0, WINDOW - s * HALF, 0)
                    sc = jnp.where(visible & (kc >= first_key), sc, NEG_INF)
                else:
                    sc = jnp.where(visible, sc, NEG_INF)
                m = jnp.maximum(jnp.max(sc, axis=-1, keepdims=True), sink_col)
                pexp = jnp.exp(sc - m)
                den = jnp.sum(pexp, axis=-1, keepdims=True) + jnp.exp(sink_col - m)
                o = jnp.dot(pexp.astype(BF16), vw, preferred_element_type=F32) / den
                for p in range(GROUP):
                    hcol = (g * GROUP + p) * HEAD_DIM
                    gate = _silu(pa_ref[r0:r0 + HALF, a_gate0 + hcol:a_gate0 + hcol + HEAD_DIM])
                    ycat_ref[r0:r0 + HALF, CONV_W + hcol:CONV_W + hcol + HEAD_DIM] = (
                        o[p * HALF:(p + 1) * HALF, :] * gate).astype(BF16)
    kbuf_ref[:, 0:WINDOW, :] = kbuf_ref[:, T:T + WINDOW, :]
    vbuf_ref[:, 0:WINDOW, :] = vbuf_ref[:, T:T + WINDOW, :]

    pr_ref[...] = jnp.dot(hb_ref[...], w_in_ref[:, COL_LRU:IN_TOTAL], preferred_element_type=F32)
    _lru_branch(T, RB, pr_ref, rbuf_ref, xc_ref, gb_ref, hcar_ref, lcw_ref, lcb_ref, wg_ref, bg_ref, lam_ref,
                ycat_ref)
    rbuf_ref[0:LRU_HALO, :] = rbuf_ref[T:T + LRU_HALO, :]

    def store(r, y):
        out_ref[0, r:r + RB, :] = y

    _out_proj(T, RB, resid_ref, ycat_ref, w_out_ref, mixed_ref, pg_ref, pb_ref, store)


def _meta_kernel(first_layer,
                 x_ref, cos_ref, s1_ref, s2_ref, lng_ref, lnb_ref, w_in_ref,
                 dw_w_ref, dw_b_ref, cg_ref, cb_ref, pw_w_ref, pw_b_ref, sink_ref,
                 lcw_ref, lcb_ref, wg_ref, bg_ref, lam_ref, w_out_ref, pg_ref, pb_ref,
                 out_ref, cmeta_ref, kmeta_ref, vmeta_ref, rxmeta_ref, hlmeta_ref,
                 hb_ref, hres_ref, pc_ref, pa_ref, pr_ref, gb_ref, mixed_ref, ycat_ref,
                 cbuf_ref, rbuf_ref, xc_ref, hcar_ref):
    M = N_META
    if first_layer:
        hres_ref[...] = _layer_norm(x_ref[...], lng_ref[...], lnb_ref[...])
    else:
        hres_ref[...] = x_ref[...]
    hb_ref[...] = hres_ref[...].astype(BF16)

    pc_ref[...] = jnp.dot(hb_ref[...], w_in_ref[:, 0:COL_ATT], preferred_element_type=F32)
    cbuf_ref[0:CONV_HALO, :] = jnp.zeros((CONV_HALO, CONV_W), F32)
    _conv_branch(M, M, pc_ref, cbuf_ref, dw_w_ref, dw_b_ref, cg_ref, cb_ref, pw_w_ref, pw_b_ref, ycat_ref)
    cmeta_ref[...] = cbuf_ref[CONV_HALO:CONV_HALO + M, :]

    pa_ref[...] = jnp.dot(hb_ref[...], w_in_ref[:, COL_ATT:COL_LRU], preferred_element_type=F32)
    cos, s1, s2 = cos_ref[...], s1_ref[...], s2_ref[...]
    qs = [_rotary(pa_ref[:, j * LANES:(j + 1) * LANES], cos, s1, s2) * ATT_SCALE for j in range(ATT_W // LANES)]
    for j in range(KV_W // LANES):
        kmeta_ref[:, j * LANES:(j + 1) * LANES] = _rotary(
            pa_ref[:, ATT_W + j * LANES:ATT_W + (j + 1) * LANES], cos, s1, s2)
    vmeta_ref[...] = pa_ref[:, ATT_W + KV_W:ATT_W + 2 * KV_W]
    nrow = GROUP * M
    qi = lax.broadcasted_iota(jnp.int32, (nrow, M), 0) % M
    kc = lax.broadcasted_iota(jnp.int32, (nrow, M), 1)
    rowp = lax.broadcasted_iota(jnp.int32, (nrow, 1), 0) // M
    a_gate0 = ATT_W + 2 * KV_W
    for g in range(N_KV):
        sink_col = jnp.zeros((nrow, 1), F32)
        for p in range(GROUP):
            sink_col = jnp.where(rowp == p, sink_ref[g * GROUP + p], sink_col)
        heads = []
        for p in range(GROUP):
            hd = g * GROUP + p
            slab = qs[hd // 2]
            heads.append(slab[:, (hd % 2) * HEAD_DIM:(hd % 2 + 1) * HEAD_DIM].astype(BF16))
        q = jnp.concatenate(heads, axis=0)
        k = kmeta_ref[:, g * HEAD_DIM:(g + 1) * HEAD_DIM].astype(BF16)
        v = vmeta_ref[:, g * HEAD_DIM:(g + 1) * HEAD_DIM].astype(BF16)
        sc = lax.dot_general(q, k, (((1,), (1,)), ((), ())), preferred_element_type=F32)
        sc = jnp.where(kc <= qi, sc, NEG_INF)
        m = jnp.maximum(jnp.max(sc, axis=-1, keepdims=True), sink_col)
        pexp = jnp.exp(sc - m)
        den = jnp.sum(pexp, axis=-1, keepdims=True) + jnp.exp(sink_col - m)
        o = jnp.dot(pexp.astype(BF16), v, preferred_element_type=F32) / den
        for p in range(GROUP):
            hcol = (g * GROUP + p) * HEAD_DIM
            gate = _silu(pa_ref[:, a_gate0 + hcol:a_gate0 + hcol + HEAD_DIM])
            ycat_ref[:, CONV_W + hcol:CONV_W + hcol + HEAD_DIM] = (o[p * M:(p + 1) * M, :] * gate).astype(BF16)

    pr_ref[...] = jnp.dot(hb_ref[...], w_in_ref[:, COL_LRU:IN_TOTAL], preferred_element_type=F32)
    rbuf_ref[0:LRU_HALO, :] = jnp.zeros((LRU_HALO, LRU_W), F32)
    hcar_ref[...] = jnp.zeros((SUBLANES, LRU_W), F32)
    rxmeta_ref[...] = pr_ref[:, 0:LRU_W]
    _lru_branch(M, M, pr_ref, rbuf_ref, xc_ref, gb_ref, hcar_ref, lcw_ref, lcb_ref, wg_ref, bg_ref, lam_ref,
                ycat_ref, hout_ref=hlmeta_ref)

    def store(r, y):
        out_ref[r:r + M, :] = y

    _out_proj(M, M, hres_ref, ycat_ref, w_out_ref, mixed_ref, pg_ref, pb_ref, store)


def _rope_tables(pos):
    half = ROT_DIM // 2
    inv_freq = ROPE_THETA ** (-jnp.arange(half, dtype=F32) / half)
    ang = pos.astype(F32)[:, None] * inv_freq[None, :]
    cos, sin = jnp.cos(ang), jnp.sin(ang)
    n = pos.shape[0]
    one = jnp.ones((n, HEAD_DIM - ROT_DIM), F32)
    zero = jnp.zeros((n, HEAD_DIM - ROT_DIM), F32)
    zh = jnp.zeros((n, half), F32)
    c64 = jnp.concatenate([cos, cos, one], axis=1)
    s1_64 = jnp.concatenate([-sin, zh, zero], axis=1)
    s2_64 = jnp.concatenate([zh, sin, zero], axis=1)
    rep = LANES // HEAD_DIM
    return jnp.tile(c64, (1, rep)), jnp.tile(s1_64, (1, rep)), jnp.tile(s2_64, (1, rep))


def _block_diag(w):
    h, d, _ = w.shape
    eye = jnp.eye(h, dtype=w.dtype)
    return (eye[:, None, :, None] * w[:, :, None, :]).reshape(h * d, h * d)


def _resident(shape):
    nd = len(shape)
    return pl.BlockSpec(shape, lambda *_: (0,) * nd, pipeline_mode=pl.Buffered(1))


def _layer_params(l, ln_in_g, ln_in_b, w_in_bf, conv_dw_w, conv_dw_b, conv_ln_g, conv_ln_b, conv_pw_bf, conv_pw_b,
                  attn_sinks, lru_conv_w, lru_conv_b, w_gate_bf, b_gate, lru_lambda, w_out_bf, ln_post_g, ln_post_b):
    row = lambda a: a.reshape(1, -1)
    return [row(ln_in_g), row(ln_in_b), w_in_bf[l],
            conv_dw_w[l], row(conv_dw_b[l]), row(conv_ln_g[l]), row(conv_ln_b[l]), conv_pw_bf[l], row(conv_pw_b[l]),
            attn_sinks[l],
            lru_conv_w[l], row(lru_conv_b[l]), w_gate_bf[l], row(b_gate[l]), row(lru_lambda[l]),
            w_out_bf[l], row(ln_post_g[l]), row(ln_post_b[l])]


def _param_specs():
    return [_resident((1, D_MODEL)), _resident((1, D_MODEL)), _resident((D_MODEL, IN_TOTAL)),
            _resident((CONV_K, CONV_W)), _resident((1, CONV_W)), _resident((1, CONV_W)), _resident((1, CONV_W)),
            _resident((CONV_W, CONV_W)), _resident((1, CONV_W)),
            pl.BlockSpec(memory_space=pltpu.SMEM),
            _resident((LRU_CONV_K, LRU_W)), _resident((1, LRU_W)), _resident((LRU_W, 2 * LRU_W)),
            _resident((1, 2 * LRU_W)), _resident((1, LRU_W)),
            _resident((OUT_IN, D_MODEL)), _resident((1, D_MODEL)), _resident((1, D_MODEL))]


def _meta_call(first_layer, h_meta, tables, params):
    M = N_META
    f = lambda *shape: jax.ShapeDtypeStruct(shape, F32)
    out_shape = [f(M, D_MODEL), f(M, CONV_W), f(M, KV_W), f(M, KV_W), f(M, LRU_W), f(M, LRU_W)]
    in_specs = [_resident((M, D_MODEL))] + [_resident((M, LANES))] * 3 + _param_specs()
    out_specs = [pl.BlockSpec(s.shape, lambda i: (0, 0)) for s in out_shape]
    scratch = [pltpu.VMEM((M, D_MODEL), BF16), pltpu.VMEM((M, D_MODEL), F32),
               pltpu.VMEM((M, COL_ATT), F32), pltpu.VMEM((M, COL_LRU - COL_ATT), F32),
               pltpu.VMEM((M, 2 * LRU_W), F32), pltpu.VMEM((M, 2 * LRU_W), F32),
               pltpu.VMEM((M, D_MODEL), F32), pltpu.VMEM((M, OUT_IN), BF16),
               pltpu.VMEM((CONV_HALO + M, CONV_W), F32), pltpu.VMEM((LRU_HALO + M, LRU_W), F32),
               pltpu.VMEM((M, LRU_W), F32), pltpu.VMEM((SUBLANES, LRU_W), F32)]
    return pl.pallas_call(
        functools.partial(_meta_kernel, first_layer),
        grid=(1,), in_specs=in_specs, out_specs=out_specs, out_shape=out_shape, scratch_shapes=scratch,
        compiler_params=pltpu.CompilerParams(dimension_semantics=("arbitrary",), vmem_limit_bytes=VMEM_LIMIT_META),
        name="meta_layer1" if first_layer else "meta_layer2",
    )(h_meta, *tables, *params)


def _main_call(first_layer, h, tables, params, meta_state):
    B, S, _ = h.shape
    T = T_CHUNK
    M = N_META
    tok = pl.BlockSpec((1, T, D_MODEL), lambda b, t: (b, t, 0))
    tab = pl.BlockSpec((T, LANES), lambda b, t: (t, 0))
    in_specs = ([tok, tab, tab, tab] + _param_specs()
                + [_resident((M, CONV_W)), _resident((M, KV_W)), _resident((M, KV_W)),
                   _resident((M, LRU_W)), _resident((M, LRU_W))])
    scratch = [pltpu.VMEM((T, D_MODEL), BF16), pltpu.VMEM((T, D_MODEL) if first_layer else (SUBLANES, LANES), F32),
               pltpu.VMEM((T, COL_ATT), F32), pltpu.VMEM((T, COL_LRU - COL_ATT), F32),
               pltpu.VMEM((T, 2 * LRU_W), F32), pltpu.VMEM((T, 2 * LRU_W), F32),
               pltpu.VMEM((T, D_MODEL), F32), pltpu.VMEM((T, OUT_IN), BF16),
               pltpu.VMEM((N_HEADS, T, HEAD_DIM), BF16),
               pltpu.VMEM((N_KV, WINDOW + T, HEAD_DIM), BF16), pltpu.VMEM((N_KV, WINDOW + T, HEAD_DIM), BF16),
               pltpu.VMEM((N_KV, KPAD - KEYS, HEAD_DIM), BF16), pltpu.VMEM((N_KV, KPAD - KEYS, HEAD_DIM), BF16),
               pltpu.VMEM((CONV_HALO + T, CONV_W), F32), pltpu.VMEM((LRU_HALO + T, LRU_W), F32),
               pltpu.VMEM((T, LRU_W), F32), pltpu.VMEM((SUBLANES, LRU_W), F32)]
    return pl.pallas_call(
        functools.partial(_main_kernel, first_layer),
        grid=(B, S // T), in_specs=in_specs, out_specs=tok,
        out_shape=jax.ShapeDtypeStruct((B, S, D_MODEL), F32), scratch_shapes=scratch,
        compiler_params=pltpu.CompilerParams(dimension_semantics=("arbitrary", "arbitrary"),
                                             vmem_limit_bytes=VMEM_LIMIT_MAIN),
        name="tokens_layer1" if first_layer else "tokens_layer2",
    )(h, *tables, *params, *meta_state)


def kernel(x, meta_tokens, ln_in_g, ln_in_b, w_in, conv_dw_w, conv_dw_b, conv_ln_g, conv_ln_b, conv_pw_w, conv_pw_b,
           attn_sinks, lru_conv_w, lru_conv_b, lru_wa, lru_ba, lru_wx, lru_bx, lru_lambda, w_out, ln_post_g,
           ln_post_b):
    B, S, D = x.shape
    assert D == D_MODEL and S % T_CHUNK == 0 and w_in.shape == (DEPTH, D_MODEL, IN_TOTAL)
    w_in_bf = w_in.astype(BF16)
    w_out_bf = w_out.astype(BF16)
    conv_pw_bf = conv_pw_w.astype(BF16)
    w_gate_bf = jnp.concatenate([jax.vmap(_block_diag)(lru_wa), jax.vmap(_block_diag)(lru_wx)], axis=-1).astype(BF16)
    b_gate = jnp.concatenate([lru_ba, lru_bx], axis=-1)
    meta_tables = _rope_tables(jnp.arange(N_META, dtype=jnp.int32))
    tok_tables = _rope_tables(N_META + jnp.arange(S, dtype=jnp.int32))

    h, h_meta = x, meta_tokens.astype(x.dtype)
    for l in range(DEPTH):
        params = _layer_params(l, ln_in_g, ln_in_b, w_in_bf, conv_dw_w, conv_dw_b, conv_ln_g, conv_ln_b, conv_pw_bf,
                               conv_pw_b, attn_sinks, lru_conv_w, lru_conv_b, w_gate_bf, b_gate, lru_lambda,
                               w_out_bf, ln_post_g, ln_post_b)
        h_meta, *meta_state = _meta_call(l == 0, h_meta, meta_tables, params)
        h = _main_call(l == 0, h, tok_tables, params, meta_state)
    return h
```

```python
import functools

import jax
import jax.numpy as jnp
from jax import lax
from jax.experimental import pallas as pl
from jax.experimental.pallas import tpu as pltpu

F32 = jnp.float32
BF16 = jnp.bfloat16

D_MODEL = 2048
N_META = 16
CONV_W = 512
CONV_K = 31
HEAD_DIM = 64
N_HEADS = 16
N_KV = 4
GROUP = N_HEADS // N_KV
ATT_W = N_HEADS * HEAD_DIM
KV_W = N_KV * HEAD_DIM
WINDOW = 128
ROT_DIM = HEAD_DIM // 4
ROPE_THETA = 500000.0
LRU_W = 512
LRU_HEADS = 8
LRU_CONV_K = 4
LRU_C = 8.0
IN_TOTAL = 3 * CONV_W + 2 * ATT_W + 2 * KV_W + 2 * LRU_W
COL_ATT = 3 * CONV_W
COL_LRU = COL_ATT + 2 * ATT_W + 2 * KV_W
OUT_IN = CONV_W + ATT_W + LRU_W
LN_EPS = 1e-5
DEPTH = 2
DEEPNORM_ALPHA = (2.0 * DEPTH) ** 0.25
NEG_INF = -1e30
ATT_SCALE = HEAD_DIM ** -0.5

LANES = 128
SUBLANES = 8
T_CHUNK = 256
CONV_HALO = 32
LRU_HALO = SUBLANES
HALF = WINDOW // 2
KEYS = WINDOW + HALF
KPAD = 256
ROW_BLOCK = 32
PROJ_CHUNK = 512
VMEM_LIMIT_MAIN = 58 * 1024 * 1024
VMEM_LIMIT_META = 48 * 1024 * 1024

def _silu(x):
    return x * jax.nn.sigmoid(x)


def _layer_norm(x, g, b):
    mu = jnp.mean(x, axis=-1, keepdims=True)
    xc = x - mu
    var = jnp.mean(xc * xc, axis=-1, keepdims=True)
    return xc * lax.rsqrt(var + LN_EPS) * g + b


def _bf16_rows(w_ref, r0, r1, c0, c1):
    return w_ref[r0:r1, c0:c1]


def _rotary(x, cos, s1, s2):
    return x * cos + pltpu.roll(x, LANES - ROT_DIM // 2, 1) * s1 + pltpu.roll(x, ROT_DIM // 2, 1) * s2


def _conv_rowblock(r, rb, pc_ref, cbuf_ref, dw_w_ref, dw_b_ref, g_ref, b_ref, cy_ref):
    accs = []
    for l in range(CONV_W // LANES):
        lo, hi = l * LANES, (l + 1) * LANES
        cbuf_ref[l, CONV_HALO + r:CONV_HALO + r + rb, :] = (
            pc_ref[r:r + rb, lo:hi] * jax.nn.sigmoid(pc_ref[r:r + rb, CONV_W + lo:CONV_W + hi]))
        acc = jnp.broadcast_to(dw_b_ref[:, lo:hi], (rb, LANES))
        for k in range(CONV_K):
            start = CONV_HALO - (CONV_K - 1) + k + r
            acc = acc + dw_w_ref[k:k + 1, lo:hi] * cbuf_ref[l, start:start + rb, :]
        accs.append(acc)
    y = _layer_norm(jnp.concatenate(accs, axis=1), g_ref[...], b_ref[...])
    cy_ref[r:r + rb, :] = _silu(y).astype(BF16)


def _conv_finish(rows, rb, pc_ref, cy_ref, pw_w_ref, pw_b_ref, ycat_ref):
    z = jnp.dot(cy_ref[...], _bf16_rows(pw_w_ref, 0, CONV_W, 0, CONV_W), preferred_element_type=F32) + pw_b_ref[...]
    for r in range(0, rows, rb):
        ycat_ref[r:r + rb, 0:CONV_W] = (z[r:r + rb] * _silu(pc_ref[r:r + rb, 2 * CONV_W:3 * CONV_W])).astype(BF16)


def _lru_conv_gates(rows, rb, pr_ref, rbuf_ref, xc_ref, gb_ref, lcw_ref, lcb_ref, wg_ref, bg_ref):
    for l in range(LRU_W // LANES):
        lo, hi = l * LANES, (l + 1) * LANES
        rbuf_ref[l, LRU_HALO:LRU_HALO + rows, :] = pr_ref[:, lo:hi]
        for r in range(0, rows, rb):
            acc = jnp.broadcast_to(lcb_ref[:, lo:hi], (rb, LANES))
            for k in range(LRU_CONV_K):
                start = LRU_HALO - (LRU_CONV_K - 1) + k + r
                acc = acc + lcw_ref[k:k + 1, lo:hi] * rbuf_ref[l, start:start + rb, :]
            xc_ref[r:r + rb, lo:hi] = acc
    gb_ref[...] = jnp.dot(xc_ref[...].astype(BF16), _bf16_rows(wg_ref, 0, LRU_W, 0, 2 * LRU_W),
                          preferred_element_type=F32) + bg_ref[...]


def _lru_group(r, carry, neg_c_softplus, pr_ref, xc_ref, gb_ref, ycat_ref, hout_ref=None):
    sub = lax.broadcasted_iota(jnp.int32, (SUBLANES, LRU_W), 0)
    x = xc_ref[r:r + SUBLANES, :]
    gate_r = jax.nn.sigmoid(gb_ref[r:r + SUBLANES, 0:LRU_W])
    gate_i = jax.nn.sigmoid(gb_ref[r:r + SUBLANES, LRU_W:2 * LRU_W])
    a = jnp.exp(gate_r * neg_c_softplus)
    u = jnp.sqrt(1.0 - a * a) * (gate_i * x)
    for d in (1, 2, 4):
        keep = sub >= d
        a_sh = jnp.where(keep, pltpu.roll(a, d, 0), 1.0)
        u_sh = jnp.where(keep, pltpu.roll(u, d, 0), 0.0)
        u = a * u_sh + u
        a = a * a_sh
    h = a * carry + u
    if hout_ref is not None:
        hout_ref[r:r + SUBLANES, :] = h
    ycat_ref[r:r + SUBLANES, CONV_W + ATT_W:OUT_IN] = (
        h * _silu(pr_ref[r:r + SUBLANES, LRU_W:2 * LRU_W])).astype(BF16)
    return jnp.broadcast_to(h[SUBLANES - 1:SUBLANES, :], (SUBLANES, LRU_W))


def _deepnorm_rows(r, rb, resid_ref, parts, g_ref, b_ref):
    y = DEEPNORM_ALPHA * resid_ref[r:r + rb, :]
    for part_ref in parts:
        y = y + part_ref[r:r + rb, :]
    return _layer_norm(y, g_ref[...], b_ref[...])


def _main_kernel(first_layer,
                 x_ref, cos_ref, s1_ref, s2_ref, lng_ref, lnb_ref, w_in_ref,
                 dw_w_ref, dw_b_ref, cg_ref, cb_ref, pw_w_ref, pw_b_ref, sink_ref,
                 lcw_ref, lcb_ref, wg_ref, bg_ref, lam_ref, w_out_ref, pg_ref, pb_ref,
                 cmeta_ref, kmeta_ref, vmeta_ref, rxmeta_ref, hlmeta_ref,
                 out_ref,
                 hb_ref, hres_ref, pc_ref, pa_ref, pr_ref, gb_ref, mix_c_ref, mix_r_ref, mix_a_ref, ycat_ref, cy_ref,
                 qbuf_ref, kbuf_ref, vbuf_ref, kmz_ref, vmz_ref, cbuf_ref, rbuf_ref, xc_ref, hcar_ref):
    T = T_CHUNK
    RB = ROW_BLOCK
    t = pl.program_id(1)

    @pl.when(t == 0)
    def _start_of_sequence():
        for l in range(CONV_W // LANES):
            cbuf_ref[l, 0:CONV_HALO - N_META, :] = jnp.zeros((CONV_HALO - N_META, LANES), F32)
            cbuf_ref[l, CONV_HALO - N_META:CONV_HALO, :] = cmeta_ref[:, l * LANES:(l + 1) * LANES]
            rbuf_ref[l, 0:LRU_HALO, :] = rxmeta_ref[N_META - LRU_HALO:N_META, l * LANES:(l + 1) * LANES]
        hcar_ref[...] = jnp.broadcast_to(hlmeta_ref[N_META - 1:N_META, :], (SUBLANES, LRU_W))
        kbuf_ref[:, 0:WINDOW, :] = jnp.zeros((N_KV, WINDOW, HEAD_DIM), BF16)
        vbuf_ref[:, 0:WINDOW, :] = jnp.zeros((N_KV, WINDOW, HEAD_DIM), BF16)
        pad = jnp.zeros((KPAD - KEYS - N_META, HEAD_DIM), BF16)
        for g in range(N_KV):
            kmz_ref[g, 0:N_META, :] = kmeta_ref[:, g * HEAD_DIM:(g + 1) * HEAD_DIM].astype(BF16)
            kmz_ref[g, N_META:KPAD - KEYS, :] = pad
            vmz_ref[g, 0:N_META, :] = vmeta_ref[:, g * HEAD_DIM:(g + 1) * HEAD_DIM].astype(BF16)
            vmz_ref[g, N_META:KPAD - KEYS, :] = pad

    def proj_chunk(dst_ref, c0, w0, n):
        dst_ref[:, c0:c0 + n] = jnp.dot(hb_ref[...], _bf16_rows(w_in_ref, 0, D_MODEL, w0, w0 + n),
                                        preferred_element_type=F32)

    if first_layer:
        for r in range(0, T, RB):
            h = _layer_norm(x_ref[0, r:r + RB, :], lng_ref[...], lnb_ref[...])
            hres_ref[r:r + RB, :] = h
            hb_ref[r:r + RB, :] = h.astype(BF16)
        resid_ref = hres_ref
    else:
        for r in range(0, T, RB):
            hb_ref[r:r + RB, :] = x_ref[0, r:r + RB, :].astype(BF16)
        resid_ref = x_ref.at[0]
    proj_chunk(pc_ref, 0, 0, COL_ATT)

    later = [(pa_ref, c, COL_ATT + c) for c in range(0, COL_LRU - COL_ATT, PROJ_CHUNK)]
    later += [(pr_ref, c, COL_LRU + c) for c in range(0, IN_TOTAL - COL_LRU, PROJ_CHUNK)]
    for i, r in enumerate(range(0, T, RB)):
        _conv_rowblock(r, RB, pc_ref, cbuf_ref, dw_w_ref, dw_b_ref, cg_ref, cb_ref, cy_ref)
        if i < len(later):
            proj_chunk(later[i][0], later[i][1], later[i][2], PROJ_CHUNK)
    for dst, c, w0 in later[T // RB:]:
        proj_chunk(dst, c, w0, PROJ_CHUNK)
    for l in range(CONV_W // LANES):
        cbuf_ref[l, 0:CONV_HALO, :] = cbuf_ref[l, T:T + CONV_HALO, :]

    _conv_finish(T, RB, pc_ref, cy_ref, pw_w_ref, pw_b_ref, ycat_ref)
    mix_c_ref[...] = jnp.dot(ycat_ref[:, 0:CONV_W], _bf16_rows(w_out_ref, 0, CONV_W, 0, D_MODEL),
                             preferred_element_type=F32)

    cos, s1, s2 = cos_ref[...], s1_ref[...], s2_ref[...]
    for j in range(ATT_W // LANES):
        rot = _rotary(pa_ref[:, j * LANES:(j + 1) * LANES], cos, s1, s2) * ATT_SCALE
        qbuf_ref[2 * j, :, :] = rot[:, 0:HEAD_DIM].astype(BF16)
        qbuf_ref[2 * j + 1, :, :] = rot[:, HEAD_DIM:LANES].astype(BF16)
    for j in range(KV_W // LANES):
        rot = _rotary(pa_ref[:, ATT_W + j * LANES:ATT_W + (j + 1) * LANES], cos, s1, s2)
        kbuf_ref[2 * j, WINDOW:WINDOW + T, :] = rot[:, 0:HEAD_DIM].astype(BF16)
        kbuf_ref[2 * j + 1, WINDOW:WINDOW + T, :] = rot[:, HEAD_DIM:LANES].astype(BF16)
        v = pa_ref[:, ATT_W + KV_W + j * LANES:ATT_W + KV_W + (j + 1) * LANES]
        vbuf_ref[2 * j, WINDOW:WINDOW + T, :] = v[:, 0:HEAD_DIM].astype(BF16)
        vbuf_ref[2 * j + 1, WINDOW:WINDOW + T, :] = v[:, HEAD_DIM:LANES].astype(BF16)

    _lru_conv_gates(T, RB, pr_ref, rbuf_ref, xc_ref, gb_ref, lcw_ref, lcb_ref, wg_ref, bg_ref)
    for l in range(LRU_W // LANES):
        rbuf_ref[l, 0:LRU_HALO, :] = rbuf_ref[l, T:T + LRU_HALO, :]

    nrow = GROUP * HALF
    qi = lax.broadcasted_iota(jnp.int32, (nrow, KPAD), 0) % HALF
    kc = lax.broadcasted_iota(jnp.int32, (nrow, KPAD), 1)
    visible = ((kc > qi) & (kc <= qi + WINDOW)) | ((kc >= KEYS) & (kc < KEYS + N_META))
    rowp = lax.broadcasted_iota(jnp.int32, (nrow, 1), 0) // HALF
    a_gate0 = ATT_W + 2 * KV_W

    def attn_sub_block(g, blk, s):
        sink_col = jnp.zeros((nrow, 1), F32)
        for p in range(GROUP):
            sink_col = jnp.where(rowp == p, sink_ref[g * GROUP + p], sink_col)
        r0 = blk * WINDOW + s * HALF
        q = jnp.concatenate([qbuf_ref[g * GROUP + p, r0:r0 + HALF, :] for p in range(GROUP)], axis=0)
        kw = jnp.concatenate([kbuf_ref[g, r0:r0 + KEYS, :], kmz_ref[g]], axis=0)
        vw = jnp.concatenate([vbuf_ref[g, r0:r0 + KEYS, :], vmz_ref[g]], axis=0)
        sc = lax.dot_general(q, kw, (((1,), (1,)), ((), ())), preferred_element_type=F32)
        if blk == 0:
            first_key = jnp.where(t == 0, WINDOW - s * HALF, 0)
            sc = jnp.where(visible & (kc >= first_key), sc, NEG_INF)
        else:
            sc = jnp.where(visible, sc, NEG_INF)
        m = jnp.maximum(jnp.max(sc, axis=-1, keepdims=True), sink_col)
        pexp = jnp.exp(sc - m)
        den = jnp.sum(pexp, axis=-1, keepdims=True) + jnp.exp(sink_col - m)
        o = jnp.dot(pexp.astype(BF16), vw, preferred_element_type=F32) / den
        for p in range(GROUP):
            hcol = (g * GROUP + p) * HEAD_DIM
            gate = _silu(pa_ref[r0:r0 + HALF, a_gate0 + hcol:a_gate0 + hcol + HEAD_DIM])
            ycat_ref[r0:r0 + HALF, CONV_W + hcol:CONV_W + hcol + HEAD_DIM] = (
                o[p * HALF:(p + 1) * HALF, :] * gate).astype(BF16)

    sub_blocks = [(g, blk, s) for g in range(N_KV) for blk in range(T // WINDOW) for s in range(2)]
    groups_per_sub_block = 2 * (T // SUBLANES) // len(sub_blocks)
    neg_c_softplus = -LRU_C * jax.nn.softplus(-lam_ref[...])
    carry = hcar_ref[...]
    r_lru = 0
    for g, blk, s in sub_blocks:
        for _ in range(groups_per_sub_block):
            if r_lru < T:
                carry = _lru_group(r_lru, carry, neg_c_softplus, pr_ref, xc_ref, gb_ref, ycat_ref)
                r_lru += SUBLANES
        if r_lru == T:
            hcar_ref[...] = carry
            mix_r_ref[...] = jnp.dot(ycat_ref[:, CONV_W + ATT_W:OUT_IN],
                                     _bf16_rows(w_out_ref, CONV_W + ATT_W, OUT_IN, 0, D_MODEL),
                                     preferred_element_type=F32)
            r_lru += SUBLANES
        attn_sub_block(g, blk, s)
    assert r_lru == T + SUBLANES
    kbuf_ref[:, 0:WINDOW, :] = kbuf_ref[:, T:T + WINDOW, :]
    vbuf_ref[:, 0:WINDOW, :] = vbuf_ref[:, T:T + WINDOW, :]

    mix_a_ref[...] = jnp.dot(ycat_ref[:, CONV_W:CONV_W + ATT_W],
                             _bf16_rows(w_out_ref, CONV_W, CONV_W + ATT_W, 0, D_MODEL),
                             preferred_element_type=F32)
    for r in range(0, T, RB):
        out_ref[0, r:r + RB, :] = _deepnorm_rows(r, RB, resid_ref, (mix_c_ref, mix_r_ref, mix_a_ref), pg_ref, pb_ref)


def _meta_kernel(first_layer,
                 x_ref, cos_ref, s1_ref, s2_ref, lng_ref, lnb_ref, w_in_ref,
                 dw_w_ref, dw_b_ref, cg_ref, cb_ref, pw_w_ref, pw_b_ref, sink_ref,
                 lcw_ref, lcb_ref, wg_ref, bg_ref, lam_ref, w_out_ref, pg_ref, pb_ref,
                 out_ref, cmeta_ref, kmeta_ref, vmeta_ref, rxmeta_ref, hlmeta_ref,
                 hb_ref, hres_ref, pc_ref, pa_ref, pr_ref, gb_ref, mixed_ref, ycat_ref, cy_ref,
                 cbuf_ref, rbuf_ref, xc_ref):
    M = N_META
    if first_layer:
        hres_ref[...] = _layer_norm(x_ref[...], lng_ref[...], lnb_ref[...])
    else:
        hres_ref[...] = x_ref[...]
    hb_ref[...] = hres_ref[...].astype(BF16)

    pc_ref[...] = jnp.dot(hb_ref[...], _bf16_rows(w_in_ref, 0, D_MODEL, 0, COL_ATT), preferred_element_type=F32)
    for l in range(CONV_W // LANES):
        cbuf_ref[l, 0:CONV_HALO, :] = jnp.zeros((CONV_HALO, LANES), F32)
    _conv_rowblock(0, M, pc_ref, cbuf_ref, dw_w_ref, dw_b_ref, cg_ref, cb_ref, cy_ref)
    _conv_finish(M, M, pc_ref, cy_ref, pw_w_ref, pw_b_ref, ycat_ref)
    for l in range(CONV_W // LANES):
        cmeta_ref[:, l * LANES:(l + 1) * LANES] = cbuf_ref[l, CONV_HALO:CONV_HALO + M, :]

    pa_ref[...] = jnp.dot(hb_ref[...], _bf16_rows(w_in_ref, 0, D_MODEL, COL_ATT, COL_LRU), preferred_element_type=F32)
    cos, s1, s2 = cos_ref[...], s1_ref[...], s2_ref[...]
    qs = [_rotary(pa_ref[:, j * LANES:(j + 1) * LANES], cos, s1, s2) * ATT_SCALE for j in range(ATT_W // LANES)]
    for j in range(KV_W // LANES):
        kmeta_ref[:, j * LANES:(j + 1) * LANES] = _rotary(
            pa_ref[:, ATT_W + j * LANES:ATT_W + (j + 1) * LANES], cos, s1, s2)
    vmeta_ref[...] = pa_ref[:, ATT_W + KV_W:ATT_W + 2 * KV_W]
    nrow = GROUP * M
    qi = lax.broadcasted_iota(jnp.int32, (nrow, M), 0) % M
    kc = lax.broadcasted_iota(jnp.int32, (nrow, M), 1)
    rowp = lax.broadcasted_iota(jnp.int32, (nrow, 1), 0) // M
    a_gate0 = ATT_W + 2 * KV_W
    for g in range(N_KV):
        sink_col = jnp.zeros((nrow, 1), F32)
        for p in range(GROUP):
            sink_col = jnp.where(rowp == p, sink_ref[g * GROUP + p], sink_col)
        heads = []
        for p in range(GROUP):
            hd = g * GROUP + p
            slab = qs[hd // 2]
            heads.append(slab[:, (hd % 2) * HEAD_DIM:(hd % 2 + 1) * HEAD_DIM].astype(BF16))
        q = jnp.concatenate(heads, axis=0)
        k = kmeta_ref[:, g * HEAD_DIM:(g + 1) * HEAD_DIM].astype(BF16)
        v = vmeta_ref[:, g * HEAD_DIM:(g + 1) * HEAD_DIM].astype(BF16)
        sc = lax.dot_general(q, k, (((1,), (1,)), ((), ())), preferred_element_type=F32)
        sc = jnp.where(kc <= qi, sc, NEG_INF)
        m = jnp.maximum(jnp.max(sc, axis=-1, keepdims=True), sink_col)
        pexp = jnp.exp(sc - m)
        den = jnp.sum(pexp, axis=-1, keepdims=True) + jnp.exp(sink_col - m)
        o = jnp.dot(pexp.astype(BF16), v, preferred_element_type=F32) / den
        for p in range(GROUP):
            hcol = (g * GROUP + p) * HEAD_DIM
            gate = _silu(pa_ref[:, a_gate0 + hcol:a_gate0 + hcol + HEAD_DIM])
            ycat_ref[:, CONV_W + hcol:CONV_W + hcol + HEAD_DIM] = (o[p * M:(p + 1) * M, :] * gate).astype(BF16)

    pr_ref[...] = jnp.dot(hb_ref[...], _bf16_rows(w_in_ref, 0, D_MODEL, COL_LRU, IN_TOTAL), preferred_element_type=F32)
    for l in range(LRU_W // LANES):
        rbuf_ref[l, 0:LRU_HALO, :] = jnp.zeros((LRU_HALO, LANES), F32)
    rxmeta_ref[...] = pr_ref[:, 0:LRU_W]
    _lru_conv_gates(M, M, pr_ref, rbuf_ref, xc_ref, gb_ref, lcw_ref, lcb_ref, wg_ref, bg_ref)
    neg_c_softplus = -LRU_C * jax.nn.softplus(-lam_ref[...])
    carry = jnp.zeros((SUBLANES, LRU_W), F32)
    for r in range(0, M, SUBLANES):
        carry = _lru_group(r, carry, neg_c_softplus, pr_ref, xc_ref, gb_ref, ycat_ref, hout_ref=hlmeta_ref)

    mixed_ref[...] = jnp.dot(ycat_ref[...], _bf16_rows(w_out_ref, 0, OUT_IN, 0, D_MODEL), preferred_element_type=F32)
    out_ref[...] = _deepnorm_rows(0, M, hres_ref, (mixed_ref,), pg_ref, pb_ref)


def _rope_tables(pos):
    half = ROT_DIM // 2
    inv_freq = ROPE_THETA ** (-jnp.arange(half, dtype=F32) / half)
    ang = pos.astype(F32)[:, None] * inv_freq[None, :]
    cos, sin = jnp.cos(ang), jnp.sin(ang)
    n = pos.shape[0]
    one = jnp.ones((n, HEAD_DIM - ROT_DIM), F32)
    zero = jnp.zeros((n, HEAD_DIM - ROT_DIM), F32)
    zh = jnp.zeros((n, half), F32)
    c64 = jnp.concatenate([cos, cos, one], axis=1)
    s1_64 = jnp.concatenate([-sin, zh, zero], axis=1)
    s2_64 = jnp.concatenate([zh, sin, zero], axis=1)
    rep = LANES // HEAD_DIM
    return jnp.tile(c64, (1, rep)), jnp.tile(s1_64, (1, rep)), jnp.tile(s2_64, (1, rep))


def _block_diag(w):
    h, d, _ = w.shape
    eye = jnp.eye(h, dtype=w.dtype)
    return (eye[:, None, :, None] * w[:, :, None, :]).reshape(h * d, h * d)


def _resident(shape):
    nd = len(shape)
    return pl.BlockSpec(shape, lambda *_: (0,) * nd, pipeline_mode=pl.Buffered(1))


def _layer_params(l, ln_in_g, ln_in_b, w_in_bf, conv_dw_w, conv_dw_b, conv_ln_g, conv_ln_b, conv_pw_bf, conv_pw_b,
                  attn_sinks, lru_conv_w, lru_conv_b, w_gate_bf, b_gate, lru_lambda, w_out_bf, ln_post_g, ln_post_b):
    row = lambda a: a.reshape(1, -1)
    return [row(ln_in_g), row(ln_in_b), w_in_bf[l],
            conv_dw_w[l], row(conv_dw_b[l]), row(conv_ln_g[l]), row(conv_ln_b[l]), conv_pw_bf[l], row(conv_pw_b[l]),
            attn_sinks[l],
            lru_conv_w[l], row(lru_conv_b[l]), w_gate_bf[l], row(b_gate[l]), row(lru_lambda[l]),
            w_out_bf[l], row(ln_post_g[l]), row(ln_post_b[l])]


def _param_specs():
    return [_resident((1, D_MODEL)), _resident((1, D_MODEL)), _resident((D_MODEL, IN_TOTAL)),
            _resident((CONV_K, CONV_W)), _resident((1, CONV_W)), _resident((1, CONV_W)), _resident((1, CONV_W)),
            _resident((CONV_W, CONV_W)), _resident((1, CONV_W)),
            pl.BlockSpec(memory_space=pltpu.SMEM),
            _resident((LRU_CONV_K, LRU_W)), _resident((1, LRU_W)), _resident((LRU_W, 2 * LRU_W)),
            _resident((1, 2 * LRU_W)), _resident((1, LRU_W)),
            _resident((OUT_IN, D_MODEL)), _resident((1, D_MODEL)), _resident((1, D_MODEL))]


def _meta_call(first_layer, h_meta, tables, params):
    M = N_META
    f = lambda *shape: jax.ShapeDtypeStruct(shape, F32)
    out_shape = [f(M, D_MODEL), f(M, CONV_W), f(M, KV_W), f(M, KV_W), f(M, LRU_W), f(M, LRU_W)]
    in_specs = [_resident((M, D_MODEL))] + [_resident((M, LANES))] * 3 + _param_specs()
    out_specs = [pl.BlockSpec(s.shape, lambda i: (0, 0)) for s in out_shape]
    scratch = [pltpu.VMEM((M, D_MODEL), BF16), pltpu.VMEM((M, D_MODEL), F32),
               pltpu.VMEM((M, COL_ATT), F32), pltpu.VMEM((M, COL_LRU - COL_ATT), F32),
               pltpu.VMEM((M, 2 * LRU_W), F32), pltpu.VMEM((M, 2 * LRU_W), F32),
               pltpu.VMEM((M, D_MODEL), F32), pltpu.VMEM((M, OUT_IN), BF16), pltpu.VMEM((M, CONV_W), BF16),
               pltpu.VMEM((CONV_W // LANES, CONV_HALO + M, LANES), F32),
               pltpu.VMEM((LRU_W // LANES, LRU_HALO + M, LANES), F32),
               pltpu.VMEM((M, LRU_W), F32)]
    return pl.pallas_call(
        functools.partial(_meta_kernel, first_layer),
        grid=(1,), in_specs=in_specs, out_specs=out_specs, out_shape=out_shape, scratch_shapes=scratch,
        compiler_params=pltpu.CompilerParams(dimension_semantics=("arbitrary",), vmem_limit_bytes=VMEM_LIMIT_META),
        name="meta_layer1" if first_layer else "meta_layer2",
    )(h_meta, *tables, *params)


def _main_call(first_layer, h, tables, params, meta_state):
    B, S, _ = h.shape
    T = T_CHUNK
    M = N_META
    tok = pl.BlockSpec((1, T, D_MODEL), lambda b, t: (b, t, 0))
    tab = pl.BlockSpec((T, LANES), lambda b, t: (t, 0))
    in_specs = ([tok, tab, tab, tab] + _param_specs()
                + [_resident((M, CONV_W)), _resident((M, KV_W)), _resident((M, KV_W)),
                   _resident((M, LRU_W)), _resident((M, LRU_W))])
    scratch = [pltpu.VMEM((T, D_MODEL), BF16), pltpu.VMEM((T, D_MODEL) if first_layer else (SUBLANES, LANES), F32),
               pltpu.VMEM((T, COL_ATT), F32), pltpu.VMEM((T, COL_LRU - COL_ATT), F32),
               pltpu.VMEM((T, 2 * LRU_W), F32), pltpu.VMEM((T, 2 * LRU_W), F32),
               pltpu.VMEM((T, D_MODEL), F32), pltpu.VMEM((T, D_MODEL), F32), pltpu.VMEM((T, D_MODEL), F32),
               pltpu.VMEM((T, OUT_IN), BF16), pltpu.VMEM((T, CONV_W), BF16),
               pltpu.VMEM((N_HEADS, T, HEAD_DIM), BF16),
               pltpu.VMEM((N_KV, WINDOW + T, HEAD_DIM), BF16), pltpu.VMEM((N_KV, WINDOW + T, HEAD_DIM), BF16),
               pltpu.VMEM((N_KV, KPAD - KEYS, HEAD_DIM), BF16), pltpu.VMEM((N_KV, KPAD - KEYS, HEAD_DIM), BF16),
               pltpu.VMEM((CONV_W // LANES, CONV_HALO + T, LANES), F32),
               pltpu.VMEM((LRU_W // LANES, LRU_HALO + T, LANES), F32),
               pltpu.VMEM((T, LRU_W), F32), pltpu.VMEM((SUBLANES, LRU_W), F32)]
    return pl.pallas_call(
        functools.partial(_main_kernel, first_layer),
        grid=(B, S // T), in_specs=in_specs, out_specs=tok,
        out_shape=jax.ShapeDtypeStruct((B, S, D_MODEL), F32), scratch_shapes=scratch,
        compiler_params=pltpu.CompilerParams(dimension_semantics=("arbitrary", "arbitrary"),
                                             vmem_limit_bytes=VMEM_LIMIT_MAIN),
        name="tokens_layer1" if first_layer else "tokens_layer2",
    )(h, *tables, *params, *meta_state)


def kernel(x, meta_tokens, ln_in_g, ln_in_b, w_in, conv_dw_w, conv_dw_b, conv_ln_g, conv_ln_b, conv_pw_w, conv_pw_b,
           attn_sinks, lru_conv_w, lru_conv_b, lru_wa, lru_ba, lru_wx, lru_bx, lru_lambda, w_out, ln_post_g,
           ln_post_b):
    B, S, D = x.shape
    assert D == D_MODEL and S % T_CHUNK == 0 and w_in.shape == (DEPTH, D_MODEL, IN_TOTAL)
    w_in_bf = w_in.astype(BF16)
    w_out_bf = w_out.astype(BF16)
    conv_pw_bf = conv_pw_w.astype(BF16)
    w_gate_bf = jnp.concatenate([jax.vmap(_block_diag)(lru_wa), jax.vmap(_block_diag)(lru_wx)], axis=-1).astype(BF16)
    b_gate = jnp.concatenate([lru_ba, lru_bx], axis=-1)
    meta_tables = _rope_tables(jnp.arange(N_META, dtype=jnp.int32))
    tok_tables = _rope_tables(N_META + jnp.arange(S, dtype=jnp.int32))

    h, h_meta = x, meta_tokens.astype(x.dtype)
    for l in range(DEPTH):
        params = _layer_params(l, ln_in_g, ln_in_b, w_in_bf, conv_dw_w, conv_dw_b, conv_ln_g, conv_ln_b, conv_pw_bf,
                               conv_pw_b, attn_sinks, lru_conv_w, lru_conv_b, w_gate_bf, b_gate, lru_lambda,
                               w_out_bf, ln_post_g, ln_post_b)
        h_meta, *meta_state = _meta_call(l == 0, h_meta, meta_tables, params)
        h = _main_call(l == 0, h, tok_tables, params, meta_state)
    return h
```

```python
import functools

import jax
import jax.numpy as jnp
from jax import lax
from jax.experimental import pallas as pl
from jax.experimental.pallas import tpu as pltpu

F32 = jnp.float32
BF16 = jnp.bfloat16

D_MODEL = 2048
N_META = 16
CONV_W = 512
CONV_K = 31
HEAD_DIM = 64
N_HEADS = 16
N_KV = 4
GROUP = N_HEADS // N_KV
ATT_W = N_HEADS * HEAD_DIM
KV_W = N_KV * HEAD_DIM
WINDOW = 128
ROT_DIM = HEAD_DIM // 4
ROPE_THETA = 500000.0
LRU_W = 512
LRU_HEADS = 8
LRU_CONV_K = 4
LRU_C = 8.0
IN_TOTAL = 3 * CONV_W + 2 * ATT_W + 2 * KV_W + 2 * LRU_W
COL_ATT = 3 * CONV_W
COL_LRU = COL_ATT + 2 * ATT_W + 2 * KV_W
OUT_IN = CONV_W + ATT_W + LRU_W
LN_EPS = 1e-5
DEPTH = 2
DEEPNORM_ALPHA = (2.0 * DEPTH) ** 0.25
NEG_INF = -1e30
ATT_SCALE = HEAD_DIM ** -0.5

LANES = 128
SUBLANES = 8
T_CHUNK = 256
CONV_HALO = 32
LRU_HALO = SUBLANES
HALF = WINDOW // 2
KEYS = WINDOW + HALF
KPAD = 256
ROW_BLOCK = 32
PROJ_CHUNK = 512
HEADS_PER_SCORE_TILE = 2
VMEM_LIMIT_MAIN = 60 * 1024 * 1024
VMEM_LIMIT_META = 48 * 1024 * 1024

def _silu(x):
    return x * jax.nn.sigmoid(x)


def _layer_norm(x, g, b):
    mu = jnp.mean(x, axis=-1, keepdims=True)
    xc = x - mu
    var = jnp.mean(xc * xc, axis=-1, keepdims=True)
    return xc * lax.rsqrt(var + LN_EPS) * g + b


def _bf16_rows(w_ref, r0, r1, c0, c1):
    return w_ref[r0:r1, c0:c1]


def _rotary(x, cos, s1, s2):
    return x * cos + pltpu.roll(x, LANES - ROT_DIM // 2, 1) * s1 + pltpu.roll(x, ROT_DIM // 2, 1) * s2


def _conv_rowblock(r, rb, pc_ref, cbuf_ref, dw_w_ref, dw_b_ref, g_ref, b_ref, cy_ref):
    accs = []
    for l in range(CONV_W // LANES):
        lo, hi = l * LANES, (l + 1) * LANES
        cbuf_ref[l, CONV_HALO + r:CONV_HALO + r + rb, :] = (
            pc_ref[r:r + rb, lo:hi] * jax.nn.sigmoid(pc_ref[r:r + rb, CONV_W + lo:CONV_W + hi]))
        acc = jnp.broadcast_to(dw_b_ref[:, lo:hi], (rb, LANES))
        for k in range(CONV_K):
            start = CONV_HALO - (CONV_K - 1) + k + r
            acc = acc + dw_w_ref[k:k + 1, lo:hi] * cbuf_ref[l, start:start + rb, :]
        accs.append(acc)
    y = _layer_norm(jnp.concatenate(accs, axis=1), g_ref[...], b_ref[...])
    cy_ref[r:r + rb, :] = _silu(y).astype(BF16)


def _conv_finish(rows, rb, pc_ref, cy_ref, pw_w_ref, pw_b_ref, ycat_ref):
    z = jnp.dot(cy_ref[...], _bf16_rows(pw_w_ref, 0, CONV_W, 0, CONV_W), preferred_element_type=F32) + pw_b_ref[...]
    for r in range(0, rows, rb):
        ycat_ref[r:r + rb, 0:CONV_W] = (z[r:r + rb] * _silu(pc_ref[r:r + rb, 2 * CONV_W:3 * CONV_W])).astype(BF16)


def _lru_conv_gates(rows, rb, pr_ref, rbuf_ref, xc_ref, gb_ref, lcw_ref, lcb_ref, wg_ref, bg_ref):
    for l in range(LRU_W // LANES):
        lo, hi = l * LANES, (l + 1) * LANES
        rbuf_ref[l, LRU_HALO:LRU_HALO + rows, :] = pr_ref[:, lo:hi]
        for r in range(0, rows, rb):
            acc = jnp.broadcast_to(lcb_ref[:, lo:hi], (rb, LANES))
            for k in range(LRU_CONV_K):
                start = LRU_HALO - (LRU_CONV_K - 1) + k + r
                acc = acc + lcw_ref[k:k + 1, lo:hi] * rbuf_ref[l, start:start + rb, :]
            xc_ref[r:r + rb, lo:hi] = acc
    gb_ref[...] = jnp.dot(xc_ref[...].astype(BF16), _bf16_rows(wg_ref, 0, LRU_W, 0, 2 * LRU_W),
                          preferred_element_type=F32) + bg_ref[...]


def _lru_group(r, carry, neg_c_softplus, pr_ref, xc_ref, gb_ref, ycat_ref, hout_ref=None):
    sub = lax.broadcasted_iota(jnp.int32, (SUBLANES, LRU_W), 0)
    x = xc_ref[r:r + SUBLANES, :]
    gate_r = jax.nn.sigmoid(gb_ref[r:r + SUBLANES, 0:LRU_W])
    gate_i = jax.nn.sigmoid(gb_ref[r:r + SUBLANES, LRU_W:2 * LRU_W])
    a = jnp.exp(gate_r * neg_c_softplus)
    u = jnp.sqrt(1.0 - a * a) * (gate_i * x)
    for d in (1, 2, 4):
        keep = sub >= d
        a_sh = jnp.where(keep, pltpu.roll(a, d, 0), 1.0)
        u_sh = jnp.where(keep, pltpu.roll(u, d, 0), 0.0)
        u = a * u_sh + u
        a = a * a_sh
    h = a * carry + u
    if hout_ref is not None:
        hout_ref[r:r + SUBLANES, :] = h
    ycat_ref[r:r + SUBLANES, CONV_W + ATT_W:OUT_IN] = (
        h * _silu(pr_ref[r:r + SUBLANES, LRU_W:2 * LRU_W])).astype(BF16)
    return jnp.broadcast_to(h[SUBLANES - 1:SUBLANES, :], (SUBLANES, LRU_W))


def _deepnorm_rows(resid, mixed, g_ref, b_ref):
    return _layer_norm(DEEPNORM_ALPHA * resid + mixed, g_ref[...], b_ref[...])


def _main_kernel(first_layer,
                 x_ref, cos_ref, s1_ref, s2_ref, lng_ref, lnb_ref, w_in_ref,
                 dw_w_ref, dw_b_ref, cg_ref, cb_ref, pw_w_ref, pw_b_ref, sink_ref,
                 lcw_ref, lcb_ref, wg_ref, bg_ref, lam_ref, w_out_ref, pg_ref, pb_ref,
                 cmeta_ref, kmeta_ref, vmeta_ref, rxmeta_ref, hlmeta_ref,
                 out_ref,
                 hb_ref, stat_ref, pc_ref, pa_ref, pr_ref, mix_ref, ycat_ref, cy_ref,
                 qbuf_ref, kbuf_ref, vbuf_ref, kmz_ref, vmz_ref, cbuf_ref, rbuf_ref, hcar_ref):
    T = T_CHUNK
    RB = ROW_BLOCK
    t = pl.program_id(1)

    @pl.when(t == 0)
    def _start_of_sequence():
        for l in range(CONV_W // LANES):
            cbuf_ref[l, 0:CONV_HALO - N_META, :] = jnp.zeros((CONV_HALO - N_META, LANES), F32)
            cbuf_ref[l, CONV_HALO - N_META:CONV_HALO, :] = cmeta_ref[:, l * LANES:(l + 1) * LANES]
            rbuf_ref[l, 0:LRU_HALO, :] = rxmeta_ref[N_META - LRU_HALO:N_META, l * LANES:(l + 1) * LANES]
        hcar_ref[...] = jnp.broadcast_to(hlmeta_ref[N_META - 1:N_META, :], (SUBLANES, LRU_W))
        kbuf_ref[:, 0:WINDOW, :] = jnp.zeros((N_KV, WINDOW, HEAD_DIM), BF16)
        vbuf_ref[:, 0:WINDOW, :] = jnp.zeros((N_KV, WINDOW, HEAD_DIM), BF16)
        pad = jnp.zeros((KPAD - KEYS - N_META, HEAD_DIM), BF16)
        for g in range(N_KV):
            kmz_ref[g, 0:N_META, :] = kmeta_ref[:, g * HEAD_DIM:(g + 1) * HEAD_DIM].astype(BF16)
            kmz_ref[g, N_META:KPAD - KEYS, :] = pad
            vmz_ref[g, 0:N_META, :] = vmeta_ref[:, g * HEAD_DIM:(g + 1) * HEAD_DIM].astype(BF16)
            vmz_ref[g, N_META:KPAD - KEYS, :] = pad

    def proj_chunk(dst_ref, c0, w0, n):
        dst_ref[:, c0:c0 + n] = jnp.dot(hb_ref[...], _bf16_rows(w_in_ref, 0, D_MODEL, w0, w0 + n),
                                        preferred_element_type=F32)

    def stream_rows(r):
        x = x_ref[0, r:r + RB, :]
        if not first_layer:
            return x
        return (x - stat_ref[r:r + RB, 0:1]) * stat_ref[r:r + RB, 1:2] * lng_ref[...] + lnb_ref[...]

    for r in range(0, T, RB):
        if first_layer:
            x = x_ref[0, r:r + RB, :]
            mu = jnp.mean(x, axis=-1, keepdims=True)
            xc = x - mu
            stat_ref[r:r + RB, 0:1] = mu
            stat_ref[r:r + RB, 1:2] = lax.rsqrt(jnp.mean(xc * xc, axis=-1, keepdims=True) + LN_EPS)
        hb_ref[r:r + RB, :] = stream_rows(r).astype(BF16)
    proj_chunk(pc_ref, 0, 0, COL_ATT)

    for r in range(0, T, RB):
        _conv_rowblock(r, RB, pc_ref, cbuf_ref, dw_w_ref, dw_b_ref, cg_ref, cb_ref, cy_ref)
    for l in range(CONV_W // LANES):
        cbuf_ref[l, 0:CONV_HALO, :] = cbuf_ref[l, T:T + CONV_HALO, :]
    for c in range(0, COL_LRU - COL_ATT, PROJ_CHUNK):
        proj_chunk(pa_ref, c, COL_ATT + c, PROJ_CHUNK)
    for c in range(0, IN_TOTAL - COL_LRU, PROJ_CHUNK):
        proj_chunk(pr_ref, c, COL_LRU + c, PROJ_CHUNK)
    _conv_finish(T, RB, pc_ref, cy_ref, pw_w_ref, pw_b_ref, ycat_ref)
    mix_ref[...] = jnp.dot(ycat_ref[:, 0:CONV_W], _bf16_rows(w_out_ref, 0, CONV_W, 0, D_MODEL),
                           preferred_element_type=F32)

    cos, s1, s2 = cos_ref[...], s1_ref[...], s2_ref[...]
    for j in range(ATT_W // LANES):
        rot = _rotary(pa_ref[:, j * LANES:(j + 1) * LANES], cos, s1, s2) * ATT_SCALE
        qbuf_ref[2 * j, :, :] = rot[:, 0:HEAD_DIM].astype(BF16)
        qbuf_ref[2 * j + 1, :, :] = rot[:, HEAD_DIM:LANES].astype(BF16)
    for j in range(KV_W // LANES):
        rot = _rotary(pa_ref[:, ATT_W + j * LANES:ATT_W + (j + 1) * LANES], cos, s1, s2)
        kbuf_ref[2 * j, WINDOW:WINDOW + T, :] = rot[:, 0:HEAD_DIM].astype(BF16)
        kbuf_ref[2 * j + 1, WINDOW:WINDOW + T, :] = rot[:, HEAD_DIM:LANES].astype(BF16)
        v = pa_ref[:, ATT_W + KV_W + j * LANES:ATT_W + KV_W + (j + 1) * LANES]
        vbuf_ref[2 * j, WINDOW:WINDOW + T, :] = v[:, 0:HEAD_DIM].astype(BF16)
        vbuf_ref[2 * j + 1, WINDOW:WINDOW + T, :] = v[:, HEAD_DIM:LANES].astype(BF16)

    gb_ref = pc_ref.at[:, 0:2 * LRU_W]
    xc_ref = pc_ref.at[:, 2 * LRU_W:2 * LRU_W + LRU_W]
    _lru_conv_gates(T, RB, pr_ref, rbuf_ref, xc_ref, gb_ref, lcw_ref, lcb_ref, wg_ref, bg_ref)
    for l in range(LRU_W // LANES):
        rbuf_ref[l, 0:LRU_HALO, :] = rbuf_ref[l, T:T + LRU_HALO, :]

    hp = HEADS_PER_SCORE_TILE
    nrow = hp * HALF
    qi = lax.broadcasted_iota(jnp.int32, (nrow, KPAD), 0) % HALF
    kc = lax.broadcasted_iota(jnp.int32, (nrow, KPAD), 1)
    visible = ((kc > qi) & (kc <= qi + WINDOW)) | ((kc >= KEYS) & (kc < KEYS + N_META))
    rowp = lax.broadcasted_iota(jnp.int32, (nrow, 1), 0) // HALF
    a_gate0 = ATT_W + 2 * KV_W

    def attn_sub_block(g, blk, s):
        r0 = blk * WINDOW + s * HALF
        kw = jnp.concatenate([kbuf_ref[g, r0:r0 + KEYS, :], kmz_ref[g]], axis=0)
        vw = jnp.concatenate([vbuf_ref[g, r0:r0 + KEYS, :], vmz_ref[g]], axis=0)
        for h0 in range(g * GROUP, (g + 1) * GROUP, hp):
            sink_col = jnp.zeros((nrow, 1), F32)
            for p in range(hp):
                sink_col = jnp.where(rowp == p, sink_ref[h0 + p], sink_col)
            q = jnp.concatenate([qbuf_ref[h0 + p, r0:r0 + HALF, :] for p in range(hp)], axis=0)
            sc = lax.dot_general(q, kw, (((1,), (1,)), ((), ())), preferred_element_type=F32)
            if blk == 0:
                first_key = jnp.where(t == 0, WINDOW - s * HALF, 0)
                sc = jnp.where(visible & (kc >= first_key), sc, NEG_INF)
            else:
                sc = jnp.where(visible, sc, NEG_INF)
            m = jnp.maximum(jnp.max(sc, axis=-1, keepdims=True), sink_col)
            pexp = jnp.exp(sc - m)
            den = jnp.sum(pexp, axis=-1, keepdims=True) + jnp.exp(sink_col - m)
            o = jnp.dot(pexp.astype(BF16), vw, preferred_element_type=F32) / den
            for p in range(hp):
                hcol = (h0 + p) * HEAD_DIM
                gate = _silu(pa_ref[r0:r0 + HALF, a_gate0 + hcol:a_gate0 + hcol + HEAD_DIM])
                ycat_ref[r0:r0 + HALF, CONV_W + hcol:CONV_W + hcol + HEAD_DIM] = (
                    o[p * HALF:(p + 1) * HALF, :] * gate).astype(BF16)

    sub_blocks = [(g, blk, s) for g in range(N_KV) for blk in range(T // WINDOW) for s in range(2)]
    groups_per_sub_block = 2 * (T // SUBLANES) // len(sub_blocks)
    neg_c_softplus = -LRU_C * jax.nn.softplus(-lam_ref[...])
    carry = hcar_ref[...]
    r_lru = 0
    for g, blk, s in sub_blocks:
        for _ in range(groups_per_sub_block):
            if r_lru < T:
                carry = _lru_group(r_lru, carry, neg_c_softplus, pr_ref, xc_ref, gb_ref, ycat_ref)
                r_lru += SUBLANES
        if r_lru == T:
            hcar_ref[...] = carry
            mix_ref[...] += jnp.dot(ycat_ref[:, CONV_W + ATT_W:OUT_IN],
                                    _bf16_rows(w_out_ref, CONV_W + ATT_W, OUT_IN, 0, D_MODEL),
                                    preferred_element_type=F32)
            r_lru += SUBLANES
        attn_sub_block(g, blk, s)
    assert r_lru == T + SUBLANES
    kbuf_ref[:, 0:WINDOW, :] = kbuf_ref[:, T:T + WINDOW, :]
    vbuf_ref[:, 0:WINDOW, :] = vbuf_ref[:, T:T + WINDOW, :]

    mix_ref[...] += jnp.dot(ycat_ref[:, CONV_W:CONV_W + ATT_W],
                            _bf16_rows(w_out_ref, CONV_W, CONV_W + ATT_W, 0, D_MODEL),
                            preferred_element_type=F32)
    for r in range(0, T, RB):
        out_ref[0, r:r + RB, :] = _deepnorm_rows(stream_rows(r), mix_ref[r:r + RB, :], pg_ref, pb_ref)


def _meta_kernel(first_layer,
                 x_ref, cos_ref, s1_ref, s2_ref, lng_ref, lnb_ref, w_in_ref,
                 dw_w_ref, dw_b_ref, cg_ref, cb_ref, pw_w_ref, pw_b_ref, sink_ref,
                 lcw_ref, lcb_ref, wg_ref, bg_ref, lam_ref, w_out_ref, pg_ref, pb_ref,
                 out_ref, cmeta_ref, kmeta_ref, vmeta_ref, rxmeta_ref, hlmeta_ref,
                 hb_ref, hres_ref, pc_ref, pa_ref, pr_ref, gb_ref, mixed_ref, ycat_ref, cy_ref,
                 cbuf_ref, rbuf_ref, xc_ref):
    M = N_META
    if first_layer:
        hres_ref[...] = _layer_norm(x_ref[...], lng_ref[...], lnb_ref[...])
    else:
        hres_ref[...] = x_ref[...]
    hb_ref[...] = hres_ref[...].astype(BF16)

    pc_ref[...] = jnp.dot(hb_ref[...], _bf16_rows(w_in_ref, 0, D_MODEL, 0, COL_ATT), preferred_element_type=F32)
    for l in range(CONV_W // LANES):
        cbuf_ref[l, 0:CONV_HALO, :] = jnp.zeros((CONV_HALO, LANES), F32)
    _conv_rowblock(0, M, pc_ref, cbuf_ref, dw_w_ref, dw_b_ref, cg_ref, cb_ref, cy_ref)
    _conv_finish(M, M, pc_ref, cy_ref, pw_w_ref, pw_b_ref, ycat_ref)
    for l in range(CONV_W // LANES):
        cmeta_ref[:, l * LANES:(l + 1) * LANES] = cbuf_ref[l, CONV_HALO:CONV_HALO + M, :]

    pa_ref[...] = jnp.dot(hb_ref[...], _bf16_rows(w_in_ref, 0, D_MODEL, COL_ATT, COL_LRU), preferred_element_type=F32)
    cos, s1, s2 = cos_ref[...], s1_ref[...], s2_ref[...]
    qs = [_rotary(pa_ref[:, j * LANES:(j + 1) * LANES], cos, s1, s2) * ATT_SCALE for j in range(ATT_W // LANES)]
    for j in range(KV_W // LANES):
        kmeta_ref[:, j * LANES:(j + 1) * LANES] = _rotary(
            pa_ref[:, ATT_W + j * LANES:ATT_W + (j + 1) * LANES], cos, s1, s2)
    vmeta_ref[...] = pa_ref[:, ATT_W + KV_W:ATT_W + 2 * KV_W]
    nrow = GROUP * M
    qi = lax.broadcasted_iota(jnp.int32, (nrow, M), 0) % M
    kc = lax.broadcasted_iota(jnp.int32, (nrow, M), 1)
    rowp = lax.broadcasted_iota(jnp.int32, (nrow, 1), 0) // M
    a_gate0 = ATT_W + 2 * KV_W
    for g in range(N_KV):
        sink_col = jnp.zeros((nrow, 1), F32)
        for p in range(GROUP):
            sink_col = jnp.where(rowp == p, sink_ref[g * GROUP + p], sink_col)
        heads = []
        for p in range(GROUP):
            hd = g * GROUP + p
            slab = qs[hd // 2]
            heads.append(slab[:, (hd % 2) * HEAD_DIM:(hd % 2 + 1) * HEAD_DIM].astype(BF16))
        q = jnp.concatenate(heads, axis=0)
        k = kmeta_ref[:, g * HEAD_DIM:(g + 1) * HEAD_DIM].astype(BF16)
        v = vmeta_ref[:, g * HEAD_DIM:(g + 1) * HEAD_DIM].astype(BF16)
        sc = lax.dot_general(q, k, (((1,), (1,)), ((), ())), preferred_element_type=F32)
        sc = jnp.where(kc <= qi, sc, NEG_INF)
        m = jnp.maximum(jnp.max(sc, axis=-1, keepdims=True), sink_col)
        pexp = jnp.exp(sc - m)
        den = jnp.sum(pexp, axis=-1, keepdims=True) + jnp.exp(sink_col - m)
        o = jnp.dot(pexp.astype(BF16), v, preferred_element_type=F32) / den
        for p in range(GROUP):
            hcol = (g * GROUP + p) * HEAD_DIM
            gate = _silu(pa_ref[:, a_gate0 + hcol:a_gate0 + hcol + HEAD_DIM])
            ycat_ref[:, CONV_W + hcol:CONV_W + hcol + HEAD_DIM] = (o[p * M:(p + 1) * M, :] * gate).astype(BF16)

    pr_ref[...] = jnp.dot(hb_ref[...], _bf16_rows(w_in_ref, 0, D_MODEL, COL_LRU, IN_TOTAL), preferred_element_type=F32)
    for l in range(LRU_W // LANES):
        rbuf_ref[l, 0:LRU_HALO, :] = jnp.zeros((LRU_HALO, LANES), F32)
    rxmeta_ref[...] = pr_ref[:, 0:LRU_W]
    _lru_conv_gates(M, M, pr_ref, rbuf_ref, xc_ref, gb_ref, lcw_ref, lcb_ref, wg_ref, bg_ref)
    neg_c_softplus = -LRU_C * jax.nn.softplus(-lam_ref[...])
    carry = jnp.zeros((SUBLANES, LRU_W), F32)
    for r in range(0, M, SUBLANES):
        carry = _lru_group(r, carry, neg_c_softplus, pr_ref, xc_ref, gb_ref, ycat_ref, hout_ref=hlmeta_ref)

    mixed_ref[...] = jnp.dot(ycat_ref[...], _bf16_rows(w_out_ref, 0, OUT_IN, 0, D_MODEL), preferred_element_type=F32)
    out_ref[...] = _deepnorm_rows(hres_ref[...], mixed_ref[...], pg_ref, pb_ref)


def _rope_tables(pos):
    half = ROT_DIM // 2
    inv_freq = ROPE_THETA ** (-jnp.arange(half, dtype=F32) / half)
    ang = pos.astype(F32)[:, None] * inv_freq[None, :]
    cos, sin = jnp.cos(ang), jnp.sin(ang)
    n = pos.shape[0]
    one = jnp.ones((n, HEAD_DIM - ROT_DIM), F32)
    zero = jnp.zeros((n, HEAD_DIM - ROT_DIM), F32)
    zh = jnp.zeros((n, half), F32)
    c64 = jnp.concatenate([cos, cos, one], axis=1)
    s1_64 = jnp.concatenate([-sin, zh, zero], axis=1)
    s2_64 = jnp.concatenate([zh, sin, zero], axis=1)
    rep = LANES // HEAD_DIM
    return jnp.tile(c64, (1, rep)), jnp.tile(s1_64, (1, rep)), jnp.tile(s2_64, (1, rep))


def _block_diag(w):
    h, d, _ = w.shape
    eye = jnp.eye(h, dtype=w.dtype)
    return (eye[:, None, :, None] * w[:, :, None, :]).reshape(h * d, h * d)


def _resident(shape):
    nd = len(shape)
    return pl.BlockSpec(shape, lambda *_: (0,) * nd, pipeline_mode=pl.Buffered(1))


def _layer_params(l, ln_in_g, ln_in_b, w_in, conv_dw_w, conv_dw_b, conv_ln_g, conv_ln_b, conv_pw_w, conv_pw_b,
                  attn_sinks, lru_conv_w, lru_conv_b, lru_wa, lru_ba, lru_wx, lru_bx, lru_lambda, w_out,
                  ln_post_g, ln_post_b):
    row = lambda a: a.reshape(1, -1)
    w_gate = jnp.concatenate([_block_diag(lru_wa[l]), _block_diag(lru_wx[l])], axis=-1).astype(BF16)
    b_gate = jnp.concatenate([lru_ba[l], lru_bx[l]], axis=-1)
    return [row(ln_in_g), row(ln_in_b), w_in[l].astype(BF16),
            conv_dw_w[l], row(conv_dw_b[l]), row(conv_ln_g[l]), row(conv_ln_b[l]),
            conv_pw_w[l].astype(BF16), row(conv_pw_b[l]),
            attn_sinks[l],
            lru_conv_w[l], row(lru_conv_b[l]), w_gate, row(b_gate), row(lru_lambda[l]),
            w_out[l].astype(BF16), row(ln_post_g[l]), row(ln_post_b[l])]


def _param_specs():
    return [_resident((1, D_MODEL)), _resident((1, D_MODEL)), _resident((D_MODEL, IN_TOTAL)),
            _resident((CONV_K, CONV_W)), _resident((1, CONV_W)), _resident((1, CONV_W)), _resident((1, CONV_W)),
            _resident((CONV_W, CONV_W)), _resident((1, CONV_W)),
            pl.BlockSpec(memory_space=pltpu.SMEM),
            _resident((LRU_CONV_K, LRU_W)), _resident((1, LRU_W)), _resident((LRU_W, 2 * LRU_W)),
            _resident((1, 2 * LRU_W)), _resident((1, LRU_W)),
            _resident((OUT_IN, D_MODEL)), _resident((1, D_MODEL)), _resident((1, D_MODEL))]


def _meta_call(first_layer, h_meta, tables, params):
    M = N_META
    f = lambda *shape: jax.ShapeDtypeStruct(shape, F32)
    out_shape = [f(M, D_MODEL), f(M, CONV_W), f(M, KV_W), f(M, KV_W), f(M, LRU_W), f(M, LRU_W)]
    in_specs = [_resident((M, D_MODEL))] + [_resident((M, LANES))] * 3 + _param_specs()
    out_specs = [pl.BlockSpec(s.shape, lambda i: (0, 0)) for s in out_shape]
    scratch = [pltpu.VMEM((M, D_MODEL), BF16), pltpu.VMEM((M, D_MODEL), F32),
               pltpu.VMEM((M, COL_ATT), F32), pltpu.VMEM((M, COL_LRU - COL_ATT), F32),
               pltpu.VMEM((M, 2 * LRU_W), F32), pltpu.VMEM((M, 2 * LRU_W), F32),
               pltpu.VMEM((M, D_MODEL), F32), pltpu.VMEM((M, OUT_IN), BF16), pltpu.VMEM((M, CONV_W), BF16),
               pltpu.VMEM((CONV_W // LANES, CONV_HALO + M, LANES), F32),
               pltpu.VMEM((LRU_W // LANES, LRU_HALO + M, LANES), F32),
               pltpu.VMEM((M, LRU_W), F32)]
    return pl.pallas_call(
        functools.partial(_meta_kernel, first_layer),
        grid=(1,), in_specs=in_specs, out_specs=out_specs, out_shape=out_shape, scratch_shapes=scratch,
        compiler_params=pltpu.CompilerParams(dimension_semantics=("arbitrary",), vmem_limit_bytes=VMEM_LIMIT_META),
        name="meta_layer1" if first_layer else "meta_layer2",
    )(h_meta, *tables, *params)


def _main_call(first_layer, h, tables, params, meta_state):
    B, S, _ = h.shape
    T = T_CHUNK
    M = N_META
    tok = pl.BlockSpec((1, T, D_MODEL), lambda b, t: (b, t, 0))
    tab = pl.BlockSpec((T, LANES), lambda b, t: (t, 0))
    in_specs = ([tok, tab, tab, tab] + _param_specs()
                + [_resident((M, CONV_W)), _resident((M, KV_W)), _resident((M, KV_W)),
                   _resident((M, LRU_W)), _resident((M, LRU_W))])
    scratch = [pltpu.VMEM((T, D_MODEL), BF16), pltpu.VMEM((T if first_layer else SUBLANES, LANES), F32),
               pltpu.VMEM((T, COL_ATT), F32), pltpu.VMEM((T, COL_LRU - COL_ATT), F32),
               pltpu.VMEM((T, 2 * LRU_W), F32),
               pltpu.VMEM((T, D_MODEL), F32), pltpu.VMEM((T, OUT_IN), BF16), pltpu.VMEM((T, CONV_W), BF16),
               pltpu.VMEM((N_HEADS, T, HEAD_DIM), BF16),
               pltpu.VMEM((N_KV, WINDOW + T, HEAD_DIM), BF16), pltpu.VMEM((N_KV, WINDOW + T, HEAD_DIM), BF16),
               pltpu.VMEM((N_KV, KPAD - KEYS, HEAD_DIM), BF16), pltpu.VMEM((N_KV, KPAD - KEYS, HEAD_DIM), BF16),
               pltpu.VMEM((CONV_W // LANES, CONV_HALO + T, LANES), F32),
               pltpu.VMEM((LRU_W // LANES, LRU_HALO + T, LANES), F32),
               pltpu.VMEM((SUBLANES, LRU_W), F32)]
    return pl.pallas_call(
        functools.partial(_main_kernel, first_layer),
        grid=(B, S // T), in_specs=in_specs, out_specs=tok,
        out_shape=jax.ShapeDtypeStruct((B, S, D_MODEL), F32), scratch_shapes=scratch,
        compiler_params=pltpu.CompilerParams(dimension_semantics=("arbitrary", "arbitrary"),
                                             vmem_limit_bytes=VMEM_LIMIT_MAIN),
        name="tokens_layer1" if first_layer else "tokens_layer2",
    )(h, *tables, *params, *meta_state)


def kernel(x, meta_tokens, ln_in_g, ln_in_b, w_in, conv_dw_w, conv_dw_b, conv_ln_g, conv_ln_b, conv_pw_w, conv_pw_b,
           attn_sinks, lru_conv_w, lru_conv_b, lru_wa, lru_ba, lru_wx, lru_bx, lru_lambda, w_out, ln_post_g,
           ln_post_b):
    B, S, D = x.shape
    assert D == D_MODEL and S % T_CHUNK == 0 and w_in.shape == (DEPTH, D_MODEL, IN_TOTAL)
    meta_tables = _rope_tables(jnp.arange(N_META, dtype=jnp.int32))
    tok_tables = _rope_tables(N_META + jnp.arange(S, dtype=jnp.int32))

    h, h_meta = x, meta_tokens.astype(x.dtype)
    for l in range(DEPTH):
        params = _layer_params(l, ln_in_g, ln_in_b, w_in, conv_dw_w, conv_dw_b, conv_ln_g, conv_ln_b, conv_pw_w,
                               conv_pw_b, attn_sinks, lru_conv_w, lru_conv_b, lru_wa, lru_ba, lru_wx, lru_bx,
                               lru_lambda, w_out, ln_post_g, ln_post_b)
        h_meta, *meta_state = _meta_call(l == 0, h_meta, meta_tables, params)
        h = _main_call(l == 0, h, tok_tables, params, meta_state)
    return h
```

```python
import functools

import jax
import jax.numpy as jnp
from jax import lax
from jax.experimental import pallas as pl
from jax.experimental.pallas import tpu as pltpu

F32 = jnp.float32
BF16 = jnp.bfloat16

D_MODEL = 2048
N_META = 16
CONV_W = 512
CONV_K = 31
HEAD_DIM = 64
N_HEADS = 16
N_KV = 4
GROUP = N_HEADS // N_KV
ATT_W = N_HEADS * HEAD_DIM
KV_W = N_KV * HEAD_DIM
WINDOW = 128
ROT_DIM = HEAD_DIM // 4
ROPE_THETA = 500000.0
LRU_W = 512
LRU_HEADS = 8
LRU_CONV_K = 4
LRU_C = 8.0
IN_TOTAL = 3 * CONV_W + 2 * ATT_W + 2 * KV_W + 2 * LRU_W
COL_ATT = 3 * CONV_W
COL_LRU = COL_ATT + 2 * ATT_W + 2 * KV_W
OUT_IN = CONV_W + ATT_W + LRU_W
LN_EPS = 1e-5
DEPTH = 2
DEEPNORM_ALPHA = (2.0 * DEPTH) ** 0.25
NEG_INF = -1e30
ATT_SCALE = HEAD_DIM ** -0.5

LANES = 128
SUBLANES = 8
T_CHUNK = 256
CONV_HALO = 32
LRU_HALO = SUBLANES
HALF = WINDOW // 2
KEYS = WINDOW + HALF
KPAD = 256
ROW_BLOCK = 32
PROJ_CHUNK = 512
HEADS_PER_SCORE_TILE = 2
VMEM_LIMIT_MAIN = 60 * 1024 * 1024
VMEM_LIMIT_META = 48 * 1024 * 1024

def _silu(x):
    return x * jax.nn.sigmoid(x)


def _layer_norm(x, g, b):
    mu = jnp.mean(x, axis=-1, keepdims=True)
    xc = x - mu
    var = jnp.mean(xc * xc, axis=-1, keepdims=True)
    return xc * lax.rsqrt(var + LN_EPS) * g + b


def _bf16_rows(w_ref, r0, r1, c0, c1):
    return w_ref[r0:r1, c0:c1]


def _rotary(x, cos, s1, s2):
    return x * cos + pltpu.roll(x, LANES - ROT_DIM // 2, 1) * s1 + pltpu.roll(x, ROT_DIM // 2, 1) * s2


def _pace_zero(x):
    return jnp.minimum(jnp.abs(x), 0.0)


def _conv_rowblock(r, rb, pc_ref, cbuf_ref, dw_w_ref, dw_b_ref, g_ref, b_ref, cy_ref, pace=None):
    accs = []
    for l in range(CONV_W // LANES):
        lo, hi = l * LANES, (l + 1) * LANES
        cbuf_ref[l, CONV_HALO + r:CONV_HALO + r + rb, :] = (
            pc_ref[r:r + rb, lo:hi] * jax.nn.sigmoid(pc_ref[r:r + rb, CONV_W + lo:CONV_W + hi]))
        acc = jnp.broadcast_to(dw_b_ref[:, lo:hi], (rb, LANES))
        if pace is not None:
            acc = acc + jnp.concatenate([pace] * (rb // SUBLANES), axis=0)
        for k in range(CONV_K):
            start = CONV_HALO - (CONV_K - 1) + k + r
            acc = acc + dw_w_ref[k:k + 1, lo:hi] * cbuf_ref[l, start:start + rb, :]
        accs.append(acc)
    y = _layer_norm(jnp.concatenate(accs, axis=1), g_ref[...], b_ref[...])
    cy_ref[r:r + rb, :] = _silu(y).astype(BF16)


def _conv_finish(rows, rb, pc_ref, cy_ref, pw_w_ref, pw_b_ref, ycat_ref):
    z = jnp.dot(cy_ref[...], _bf16_rows(pw_w_ref, 0, CONV_W, 0, CONV_W), preferred_element_type=F32) + pw_b_ref[...]
    for r in range(0, rows, rb):
        ycat_ref[r:r + rb, 0:CONV_W] = (z[r:r + rb] * _silu(pc_ref[r:r + rb, 2 * CONV_W:3 * CONV_W])).astype(BF16)


def _lru_conv_gates(rows, rb, pr_ref, rbuf_ref, xc_ref, gb_ref, lcw_ref, lcb_ref, wg_ref, bg_ref):
    for l in range(LRU_W // LANES):
        lo, hi = l * LANES, (l + 1) * LANES
        rbuf_ref[l, LRU_HALO:LRU_HALO + rows, :] = pr_ref[:, lo:hi]
        for r in range(0, rows, rb):
            acc = jnp.broadcast_to(lcb_ref[:, lo:hi], (rb, LANES))
            for k in range(LRU_CONV_K):
                start = LRU_HALO - (LRU_CONV_K - 1) + k + r
                acc = acc + lcw_ref[k:k + 1, lo:hi] * rbuf_ref[l, start:start + rb, :]
            xc_ref[r:r + rb, lo:hi] = acc
    gb_ref[...] = jnp.dot(xc_ref[...].astype(BF16), _bf16_rows(wg_ref, 0, LRU_W, 0, 2 * LRU_W),
                          preferred_element_type=F32) + bg_ref[...]


def _lru_group(r, carry, neg_c_softplus, pr_ref, xc_ref, gb_ref, ycat_ref, hout_ref=None):
    sub = lax.broadcasted_iota(jnp.int32, (SUBLANES, LRU_W), 0)
    x = xc_ref[r:r + SUBLANES, :]
    gate_r = jax.nn.sigmoid(gb_ref[r:r + SUBLANES, 0:LRU_W])
    gate_i = jax.nn.sigmoid(gb_ref[r:r + SUBLANES, LRU_W:2 * LRU_W])
    a = jnp.exp(gate_r * neg_c_softplus)
    u = jnp.sqrt(1.0 - a * a) * (gate_i * x)
    for d in (1, 2, 4):
        keep = sub >= d
        a_sh = jnp.where(keep, pltpu.roll(a, d, 0), 1.0)
        u_sh = jnp.where(keep, pltpu.roll(u, d, 0), 0.0)
        u = a * u_sh + u
        a = a * a_sh
    h = a * carry + u
    if hout_ref is not None:
        hout_ref[r:r + SUBLANES, :] = h
    ycat_ref[r:r + SUBLANES, CONV_W + ATT_W:OUT_IN] = (
        h * _silu(pr_ref[r:r + SUBLANES, LRU_W:2 * LRU_W])).astype(BF16)
    return jnp.broadcast_to(h[SUBLANES - 1:SUBLANES, :], (SUBLANES, LRU_W))


def _deepnorm_rows(resid, mixed, g_ref, b_ref):
    return _layer_norm(DEEPNORM_ALPHA * resid + mixed, g_ref[...], b_ref[...])


def _main_kernel(first_layer,
                 x_ref, cos_ref, s1_ref, s2_ref, lng_ref, lnb_ref, w_in_ref,
                 dw_w_ref, dw_b_ref, cg_ref, cb_ref, pw_w_ref, pw_b_ref, sink_ref,
                 lcw_ref, lcb_ref, wg_ref, bg_ref, lam_ref, w_out_ref, pg_ref, pb_ref,
                 cmeta_ref, kmeta_ref, vmeta_ref, rxmeta_ref, hlmeta_ref,
                 out_ref,
                 hb_ref, stat_ref, pc_ref, pa_ref, pr_ref, mix_ref, ycat_ref, cy_ref,
                 qbuf_ref, kbuf_ref, vbuf_ref, kmz_ref, vmz_ref, cbuf_ref, rbuf_ref, hcar_ref):
    T = T_CHUNK
    RB = ROW_BLOCK
    t = pl.program_id(1)

    @pl.when(t == 0)
    def _start_of_sequence():
        for l in range(CONV_W // LANES):
            cbuf_ref[l, 0:CONV_HALO - N_META, :] = jnp.zeros((CONV_HALO - N_META, LANES), F32)
            cbuf_ref[l, CONV_HALO - N_META:CONV_HALO, :] = cmeta_ref[:, l * LANES:(l + 1) * LANES]
            rbuf_ref[l, 0:LRU_HALO, :] = rxmeta_ref[N_META - LRU_HALO:N_META, l * LANES:(l + 1) * LANES]
        hcar_ref[...] = jnp.broadcast_to(hlmeta_ref[N_META - 1:N_META, :], (SUBLANES, LRU_W))
        kbuf_ref[:, 0:WINDOW, :] = jnp.zeros((N_KV, WINDOW, HEAD_DIM), BF16)
        vbuf_ref[:, 0:WINDOW, :] = jnp.zeros((N_KV, WINDOW, HEAD_DIM), BF16)
        pad = jnp.zeros((KPAD - KEYS - N_META, HEAD_DIM), BF16)
        for g in range(N_KV):
            kmz_ref[g, 0:N_META, :] = kmeta_ref[:, g * HEAD_DIM:(g + 1) * HEAD_DIM].astype(BF16)
            kmz_ref[g, N_META:KPAD - KEYS, :] = pad
            vmz_ref[g, 0:N_META, :] = vmeta_ref[:, g * HEAD_DIM:(g + 1) * HEAD_DIM].astype(BF16)
            vmz_ref[g, N_META:KPAD - KEYS, :] = pad

    def proj_chunk(dst_ref, c0, w0, n):
        res = jnp.dot(hb_ref[...], _bf16_rows(w_in_ref, 0, D_MODEL, w0, w0 + n), preferred_element_type=F32)
        dst_ref[:, c0:c0 + n] = res
        return _pace_zero(res[0:SUBLANES, 0:LANES])

    def stream_rows(r):
        x = x_ref[0, r:r + RB, :]
        if not first_layer:
            return x
        return (x - stat_ref[r:r + RB, 0:1]) * stat_ref[r:r + RB, 1:2] * lng_ref[...] + lnb_ref[...]

    for r in range(0, T, RB):
        if first_layer:
            x = x_ref[0, r:r + RB, :]
            mu = jnp.mean(x, axis=-1, keepdims=True)
            xc = x - mu
            stat_ref[r:r + RB, 0:1] = mu
            stat_ref[r:r + RB, 1:2] = lax.rsqrt(jnp.mean(xc * xc, axis=-1, keepdims=True) + LN_EPS)
        hb_ref[r:r + RB, :] = stream_rows(r).astype(BF16)
    proj_chunk(pc_ref, 0, 0, COL_ATT)

    later = [(pa_ref, c, COL_ATT + c) for c in range(0, COL_LRU - COL_ATT, PROJ_CHUNK)]
    later += [(pr_ref, c, COL_LRU + c) for c in range(0, IN_TOTAL - COL_LRU, PROJ_CHUNK)]
    pace = None
    for i, r in enumerate(range(0, T, RB)):
        _conv_rowblock(r, RB, pc_ref, cbuf_ref, dw_w_ref, dw_b_ref, cg_ref, cb_ref, cy_ref, pace)
        if i < len(later):
            pace = proj_chunk(later[i][0], later[i][1], later[i][2], PROJ_CHUNK)
    for dst, c, w0 in later[T // RB:]:
        proj_chunk(dst, c, w0, PROJ_CHUNK)
    for l in range(CONV_W // LANES):
        cbuf_ref[l, 0:CONV_HALO, :] = cbuf_ref[l, T:T + CONV_HALO, :]
    _conv_finish(T, RB, pc_ref, cy_ref, pw_w_ref, pw_b_ref, ycat_ref)
    mix_ref[...] = jnp.dot(ycat_ref[:, 0:CONV_W], _bf16_rows(w_out_ref, 0, CONV_W, 0, D_MODEL),
                           preferred_element_type=F32)

    cos, s1, s2 = cos_ref[...], s1_ref[...], s2_ref[...]
    for j in range(ATT_W // LANES):
        rot = _rotary(pa_ref[:, j * LANES:(j + 1) * LANES], cos, s1, s2) * ATT_SCALE
        qbuf_ref[2 * j, :, :] = rot[:, 0:HEAD_DIM].astype(BF16)
        qbuf_ref[2 * j + 1, :, :] = rot[:, HEAD_DIM:LANES].astype(BF16)
    for j in range(KV_W // LANES):
        rot = _rotary(pa_ref[:, ATT_W + j * LANES:ATT_W + (j + 1) * LANES], cos, s1, s2)
        kbuf_ref[2 * j, WINDOW:WINDOW + T, :] = rot[:, 0:HEAD_DIM].astype(BF16)
        kbuf_ref[2 * j + 1, WINDOW:WINDOW + T, :] = rot[:, HEAD_DIM:LANES].astype(BF16)
        v = pa_ref[:, ATT_W + KV_W + j * LANES:ATT_W + KV_W + (j + 1) * LANES]
        vbuf_ref[2 * j, WINDOW:WINDOW + T, :] = v[:, 0:HEAD_DIM].astype(BF16)
        vbuf_ref[2 * j + 1, WINDOW:WINDOW + T, :] = v[:, HEAD_DIM:LANES].astype(BF16)

    gb_ref = pc_ref.at[:, 0:2 * LRU_W]
    xc_ref = pc_ref.at[:, 2 * LRU_W:2 * LRU_W + LRU_W]
    _lru_conv_gates(T, RB, pr_ref, rbuf_ref, xc_ref, gb_ref, lcw_ref, lcb_ref, wg_ref, bg_ref)
    for l in range(LRU_W // LANES):
        rbuf_ref[l, 0:LRU_HALO, :] = rbuf_ref[l, T:T + LRU_HALO, :]

    hp = HEADS_PER_SCORE_TILE
    nrow = hp * HALF
    qi = lax.broadcasted_iota(jnp.int32, (nrow, KPAD), 0) % HALF
    kc = lax.broadcasted_iota(jnp.int32, (nrow, KPAD), 1)
    visible = ((kc > qi) & (kc <= qi + WINDOW)) | ((kc >= KEYS) & (kc < KEYS + N_META))
    rowp = lax.broadcasted_iota(jnp.int32, (nrow, 1), 0) // HALF
    a_gate0 = ATT_W + 2 * KV_W

    def attn_sub_block(g, blk, s):
        r0 = blk * WINDOW + s * HALF
        kw = jnp.concatenate([kbuf_ref[g, r0:r0 + KEYS, :], kmz_ref[g]], axis=0)
        vw = jnp.concatenate([vbuf_ref[g, r0:r0 + KEYS, :], vmz_ref[g]], axis=0)
        for h0 in range(g * GROUP, (g + 1) * GROUP, hp):
            sink_col = jnp.zeros((nrow, 1), F32)
            for p in range(hp):
                sink_col = jnp.where(rowp == p, sink_ref[h0 + p], sink_col)
            q = jnp.concatenate([qbuf_ref[h0 + p, r0:r0 + HALF, :] for p in range(hp)], axis=0)
            sc = lax.dot_general(q, kw, (((1,), (1,)), ((), ())), preferred_element_type=F32)
            if blk == 0:
                first_key = jnp.where(t == 0, WINDOW - s * HALF, 0)
                sc = jnp.where(visible & (kc >= first_key), sc, NEG_INF)
            else:
                sc = jnp.where(visible, sc, NEG_INF)
            m = jnp.maximum(jnp.max(sc, axis=-1, keepdims=True), sink_col)
            pexp = jnp.exp(sc - m)
            den = jnp.sum(pexp, axis=-1, keepdims=True) + jnp.exp(sink_col - m)
            o = jnp.dot(pexp.astype(BF16), vw, preferred_element_type=F32) / den
            for p in range(hp):
                hcol = (h0 + p) * HEAD_DIM
                gate = _silu(pa_ref[r0:r0 + HALF, a_gate0 + hcol:a_gate0 + hcol + HEAD_DIM])
                ycat_ref[r0:r0 + HALF, CONV_W + hcol:CONV_W + hcol + HEAD_DIM] = (
                    o[p * HALF:(p + 1) * HALF, :] * gate).astype(BF16)

    sub_blocks = [(g, blk, s) for g in range(N_KV) for blk in range(T // WINDOW) for s in range(2)]
    groups_per_sub_block = 2 * (T // SUBLANES) // len(sub_blocks)
    neg_c_softplus = -LRU_C * jax.nn.softplus(-lam_ref[...])
    carry = hcar_ref[...]
    r_lru = 0
    for g, blk, s in sub_blocks:
        for _ in range(groups_per_sub_block):
            if r_lru < T:
                carry = _lru_group(r_lru, carry, neg_c_softplus, pr_ref, xc_ref, gb_ref, ycat_ref)
                r_lru += SUBLANES
        if r_lru == T:
            hcar_ref[...] = carry
            mix_ref[...] += jnp.dot(ycat_ref[:, CONV_W + ATT_W:OUT_IN],
                                    _bf16_rows(w_out_ref, CONV_W + ATT_W, OUT_IN, 0, D_MODEL),
                                    preferred_element_type=F32)
            r_lru += SUBLANES
        attn_sub_block(g, blk, s)
    assert r_lru == T + SUBLANES
    kbuf_ref[:, 0:WINDOW, :] = kbuf_ref[:, T:T + WINDOW, :]
    vbuf_ref[:, 0:WINDOW, :] = vbuf_ref[:, T:T + WINDOW, :]

    mix_ref[...] += jnp.dot(ycat_ref[:, CONV_W:CONV_W + ATT_W],
                            _bf16_rows(w_out_ref, CONV_W, CONV_W + ATT_W, 0, D_MODEL),
                            preferred_element_type=F32)
    for r in range(0, T, RB):
        out_ref[0, r:r + RB, :] = _deepnorm_rows(stream_rows(r), mix_ref[r:r + RB, :], pg_ref, pb_ref)


def _meta_kernel(first_layer,
                 x_ref, cos_ref, s1_ref, s2_ref, lng_ref, lnb_ref, w_in_ref,
                 dw_w_ref, dw_b_ref, cg_ref, cb_ref, pw_w_ref, pw_b_ref, sink_ref,
                 lcw_ref, lcb_ref, wg_ref, bg_ref, lam_ref, w_out_ref, pg_ref, pb_ref,
                 out_ref, cmeta_ref, kmeta_ref, vmeta_ref, rxmeta_ref, hlmeta_ref,
                 hb_ref, hres_ref, pc_ref, pa_ref, pr_ref, gb_ref, mixed_ref, ycat_ref, cy_ref,
                 cbuf_ref, rbuf_ref, xc_ref):
    M = N_META
    if first_layer:
        hres_ref[...] = _layer_norm(x_ref[...], lng_ref[...], lnb_ref[...])
    else:
        hres_ref[...] = x_ref[...]
    hb_ref[...] = hres_ref[...].astype(BF16)

    pc_ref[...] = jnp.dot(hb_ref[...], _bf16_rows(w_in_ref, 0, D_MODEL, 0, COL_ATT), preferred_element_type=F32)
    for l in range(CONV_W // LANES):
        cbuf_ref[l, 0:CONV_HALO, :] = jnp.zeros((CONV_HALO, LANES), F32)
    _conv_rowblock(0, M, pc_ref, cbuf_ref, dw_w_ref, dw_b_ref, cg_ref, cb_ref, cy_ref)
    _conv_finish(M, M, pc_ref, cy_ref, pw_w_ref, pw_b_ref, ycat_ref)
    for l in range(CONV_W // LANES):
        cmeta_ref[:, l * LANES:(l + 1) * LANES] = cbuf_ref[l, CONV_HALO:CONV_HALO + M, :]

    pa_ref[...] = jnp.dot(hb_ref[...], _bf16_rows(w_in_ref, 0, D_MODEL, COL_ATT, COL_LRU), preferred_element_type=F32)
    cos, s1, s2 = cos_ref[...], s1_ref[...], s2_ref[...]
    qs = [_rotary(pa_ref[:, j * LANES:(j + 1) * LANES], cos, s1, s2) * ATT_SCALE for j in range(ATT_W // LANES)]
    for j in range(KV_W // LANES):
        kmeta_ref[:, j * LANES:(j + 1) * LANES] = _rotary(
            pa_ref[:, ATT_W + j * LANES:ATT_W + (j + 1) * LANES], cos, s1, s2)
    vmeta_ref[...] = pa_ref[:, ATT_W + KV_W:ATT_W + 2 * KV_W]
    nrow = GROUP * M
    qi = lax.broadcasted_iota(jnp.int32, (nrow, M), 0) % M
    kc = lax.broadcasted_iota(jnp.int32, (nrow, M), 1)
    rowp = lax.broadcasted_iota(jnp.int32, (nrow, 1), 0) // M
    a_gate0 = ATT_W + 2 * KV_W
    for g in range(N_KV):
        sink_col = jnp.zeros((nrow, 1), F32)
        for p in range(GROUP):
            sink_col = jnp.where(rowp == p, sink_ref[g * GROUP + p], sink_col)
        heads = []
        for p in range(GROUP):
            hd = g * GROUP + p
            slab = qs[hd // 2]
            heads.append(slab[:, (hd % 2) * HEAD_DIM:(hd % 2 + 1) * HEAD_DIM].astype(BF16))
        q = jnp.concatenate(heads, axis=0)
        k = kmeta_ref[:, g * HEAD_DIM:(g + 1) * HEAD_DIM].astype(BF16)
        v = vmeta_ref[:, g * HEAD_DIM:(g + 1) * HEAD_DIM].astype(BF16)
        sc = lax.dot_general(q, k, (((1,), (1,)), ((), ())), preferred_element_type=F32)
        sc = jnp.where(kc <= qi, sc, NEG_INF)
        m = jnp.maximum(jnp.max(sc, axis=-1, keepdims=True), sink_col)
        pexp = jnp.exp(sc - m)
        den = jnp.sum(pexp, axis=-1, keepdims=True) + jnp.exp(sink_col - m)
        o = jnp.dot(pexp.astype(BF16), v, preferred_element_type=F32) / den
        for p in range(GROUP):
            hcol = (g * GROUP + p) * HEAD_DIM
            gate = _silu(pa_ref[:, a_gate0 + hcol:a_gate0 + hcol + HEAD_DIM])
            ycat_ref[:, CONV_W + hcol:CONV_W + hcol + HEAD_DIM] = (o[p * M:(p + 1) * M, :] * gate).astype(BF16)

    pr_ref[...] = jnp.dot(hb_ref[...], _bf16_rows(w_in_ref, 0, D_MODEL, COL_LRU, IN_TOTAL), preferred_element_type=F32)
    for l in range(LRU_W // LANES):
        rbuf_ref[l, 0:LRU_HALO, :] = jnp.zeros((LRU_HALO, LANES), F32)
    rxmeta_ref[...] = pr_ref[:, 0:LRU_W]
    _lru_conv_gates(M, M, pr_ref, rbuf_ref, xc_ref, gb_ref, lcw_ref, lcb_ref, wg_ref, bg_ref)
    neg_c_softplus = -LRU_C * jax.nn.softplus(-lam_ref[...])
    carry = jnp.zeros((SUBLANES, LRU_W), F32)
    for r in range(0, M, SUBLANES):
        carry = _lru_group(r, carry, neg_c_softplus, pr_ref, xc_ref, gb_ref, ycat_ref, hout_ref=hlmeta_ref)

    mixed_ref[...] = jnp.dot(ycat_ref[...], _bf16_rows(w_out_ref, 0, OUT_IN, 0, D_MODEL), preferred_element_type=F32)
    out_ref[...] = _deepnorm_rows(hres_ref[...], mixed_ref[...], pg_ref, pb_ref)


def _rope_tables(pos):
    half = ROT_DIM // 2
    inv_freq = ROPE_THETA ** (-jnp.arange(half, dtype=F32) / half)
    ang = pos.astype(F32)[:, None] * inv_freq[None, :]
    cos, sin = jnp.cos(ang), jnp.sin(ang)
    n = pos.shape[0]
    one = jnp.ones((n, HEAD_DIM - ROT_DIM), F32)
    zero = jnp.zeros((n, HEAD_DIM - ROT_DIM), F32)
    zh = jnp.zeros((n, half), F32)
    c64 = jnp.concatenate([cos, cos, one], axis=1)
    s1_64 = jnp.concatenate([-sin, zh, zero], axis=1)
    s2_64 = jnp.concatenate([zh, sin, zero], axis=1)
    rep = LANES // HEAD_DIM
    return jnp.tile(c64, (1, rep)), jnp.tile(s1_64, (1, rep)), jnp.tile(s2_64, (1, rep))


def _block_diag(w):
    h, d, _ = w.shape
    eye = jnp.eye(h, dtype=w.dtype)
    return (eye[:, None, :, None] * w[:, :, None, :]).reshape(h * d, h * d)


def _resident(shape):
    nd = len(shape)
    return pl.BlockSpec(shape, lambda *_: (0,) * nd, pipeline_mode=pl.Buffered(1))


def _layer_params(l, ln_in_g, ln_in_b, w_in, conv_dw_w, conv_dw_b, conv_ln_g, conv_ln_b, conv_pw_w, conv_pw_b,
                  attn_sinks, lru_conv_w, lru_conv_b, lru_wa, lru_ba, lru_wx, lru_bx, lru_lambda, w_out,
                  ln_post_g, ln_post_b):
    row = lambda a: a.reshape(1, -1)
    w_gate = jnp.concatenate([_block_diag(lru_wa[l]), _block_diag(lru_wx[l])], axis=-1).astype(BF16)
    b_gate = jnp.concatenate([lru_ba[l], lru_bx[l]], axis=-1)
    return [row(ln_in_g), row(ln_in_b), w_in[l].astype(BF16),
            conv_dw_w[l], row(conv_dw_b[l]), row(conv_ln_g[l]), row(conv_ln_b[l]),
            conv_pw_w[l].astype(BF16), row(conv_pw_b[l]),
            attn_sinks[l],
            lru_conv_w[l], row(lru_conv_b[l]), w_gate, row(b_gate), row(lru_lambda[l]),
            w_out[l].astype(BF16), row(ln_post_g[l]), row(ln_post_b[l])]


def _param_specs():
    return [_resident((1, D_MODEL)), _resident((1, D_MODEL)), _resident((D_MODEL, IN_TOTAL)),
            _resident((CONV_K, CONV_W)), _resident((1, CONV_W)), _resident((1, CONV_W)), _resident((1, CONV_W)),
            _resident((CONV_W, CONV_W)), _resident((1, CONV_W)),
            pl.BlockSpec(memory_space=pltpu.SMEM),
            _resident((LRU_CONV_K, LRU_W)), _resident((1, LRU_W)), _resident((LRU_W, 2 * LRU_W)),
            _resident((1, 2 * LRU_W)), _resident((1, LRU_W)),
            _resident((OUT_IN, D_MODEL)), _resident((1, D_MODEL)), _resident((1, D_MODEL))]


def _meta_call(first_layer, h_meta, tables, params):
    M = N_META
    f = lambda *shape: jax.ShapeDtypeStruct(shape, F32)
    out_shape = [f(M, D_MODEL), f(M, CONV_W), f(M, KV_W), f(M, KV_W), f(M, LRU_W), f(M, LRU_W)]
    in_specs = [_resident((M, D_MODEL))] + [_resident((M, LANES))] * 3 + _param_specs()
    out_specs = [pl.BlockSpec(s.shape, lambda i: (0, 0)) for s in out_shape]
    scratch = [pltpu.VMEM((M, D_MODEL), BF16), pltpu.VMEM((M, D_MODEL), F32),
               pltpu.VMEM((M, COL_ATT), F32), pltpu.VMEM((M, COL_LRU - COL_ATT), F32),
               pltpu.VMEM((M, 2 * LRU_W), F32), pltpu.VMEM((M, 2 * LRU_W), F32),
               pltpu.VMEM((M, D_MODEL), F32), pltpu.VMEM((M, OUT_IN), BF16), pltpu.VMEM((M, CONV_W), BF16),
               pltpu.VMEM((CONV_W // LANES, CONV_HALO + M, LANES), F32),
               pltpu.VMEM((LRU_W // LANES, LRU_HALO + M, LANES), F32),
               pltpu.VMEM((M, LRU_W), F32)]
    return pl.pallas_call(
        functools.partial(_meta_kernel, first_layer),
        grid=(1,), in_specs=in_specs, out_specs=out_specs, out_shape=out_shape, scratch_shapes=scratch,
        compiler_params=pltpu.CompilerParams(dimension_semantics=("arbitrary",), vmem_limit_bytes=VMEM_LIMIT_META),
        name="meta_layer1" if first_layer else "meta_layer2",
    )(h_meta, *tables, *params)


def _main_call(first_layer, h, tables, params, meta_state):
    B, S, _ = h.shape
    T = T_CHUNK
    M = N_META
    tok = pl.BlockSpec((1, T, D_MODEL), lambda b, t: (b, t, 0))
    tab = pl.BlockSpec((T, LANES), lambda b, t: (t, 0))
    in_specs = ([tok, tab, tab, tab] + _param_specs()
                + [_resident((M, CONV_W)), _resident((M, KV_W)), _resident((M, KV_W)),
                   _resident((M, LRU_W)), _resident((M, LRU_W))])
    scratch = [pltpu.VMEM((T, D_MODEL), BF16), pltpu.VMEM((T if first_layer else SUBLANES, LANES), F32),
               pltpu.VMEM((T, COL_ATT), F32), pltpu.VMEM((T, COL_LRU - COL_ATT), F32),
               pltpu.VMEM((T, 2 * LRU_W), F32),
               pltpu.VMEM((T, D_MODEL), F32), pltpu.VMEM((T, OUT_IN), BF16), pltpu.VMEM((T, CONV_W), BF16),
               pltpu.VMEM((N_HEADS, T, HEAD_DIM), BF16),
               pltpu.VMEM((N_KV, WINDOW + T, HEAD_DIM), BF16), pltpu.VMEM((N_KV, WINDOW + T, HEAD_DIM), BF16),
               pltpu.VMEM((N_KV, KPAD - KEYS, HEAD_DIM), BF16), pltpu.VMEM((N_KV, KPAD - KEYS, HEAD_DIM), BF16),
               pltpu.VMEM((CONV_W // LANES, CONV_HALO + T, LANES), F32),
               pltpu.VMEM((LRU_W // LANES, LRU_HALO + T, LANES), F32),
               pltpu.VMEM((SUBLANES, LRU_W), F32)]
    return pl.pallas_call(
        functools.partial(_main_kernel, first_layer),
        grid=(B, S // T), in_specs=in_specs, out_specs=tok,
        out_shape=jax.ShapeDtypeStruct((B, S, D_MODEL), F32), scratch_shapes=scratch,
        compiler_params=pltpu.CompilerParams(dimension_semantics=("arbitrary", "arbitrary"),
                                             vmem_limit_bytes=VMEM_LIMIT_MAIN),
        name="tokens_layer1" if first_layer else "tokens_layer2",
    )(h, *tables, *params, *meta_state)


def kernel(x, meta_tokens, ln_in_g, ln_in_b, w_in, conv_dw_w, conv_dw_b, conv_ln_g, conv_ln_b, conv_pw_w, conv_pw_b,
           attn_sinks, lru_conv_w, lru_conv_b, lru_wa, lru_ba, lru_wx, lru_bx, lru_lambda, w_out, ln_post_g,
           ln_post_b):
    B, S, D = x.shape
    assert D == D_MODEL and S % T_CHUNK == 0 and w_in.shape == (DEPTH, D_MODEL, IN_TOTAL)
    meta_tables = _rope_tables(jnp.arange(N_META, dtype=jnp.int32))
    tok_tables = _rope_tables(N_META + jnp.arange(S, dtype=jnp.int32))

    h, h_meta = x, meta_tokens.astype(x.dtype)
    for l in range(DEPTH):
        params = _layer_params(l, ln_in_g, ln_in_b, w_in, conv_dw_w, conv_dw_b, conv_ln_g, conv_ln_b, conv_pw_w,
                               conv_pw_b, attn_sinks, lru_conv_w, lru_conv_b, lru_wa, lru_ba, lru_wx, lru_bx,
                               lru_lambda, w_out, ln_post_g, ln_post_b)
        h_meta, *meta_state = _meta_call(l == 0, h_meta, meta_tables, params)
        h = _main_call(l == 0, h, tok_tables, params, meta_state)
    return h
```

```python
import functools

import jax
import jax.numpy as jnp
from jax import lax
from jax.experimental import pallas as pl
from jax.experimental.pallas import tpu as pltpu

F32 = jnp.float32
BF16 = jnp.bfloat16

D_MODEL = 2048
N_META = 16
CONV_W = 512
CONV_K = 31
HEAD_DIM = 64
N_HEADS = 16
N_KV = 4
GROUP = N_HEADS // N_KV
ATT_W = N_HEADS * HEAD_DIM
KV_W = N_KV * HEAD_DIM
WINDOW = 128
ROT_DIM = HEAD_DIM // 4
ROPE_THETA = 500000.0
LRU_W = 512
LRU_HEADS = 8
LRU_CONV_K = 4
LRU_C = 8.0
IN_TOTAL = 3 * CONV_W + 2 * ATT_W + 2 * KV_W + 2 * LRU_W
COL_ATT = 3 * CONV_W
COL_LRU = COL_ATT + 2 * ATT_W + 2 * KV_W
OUT_IN = CONV_W + ATT_W + LRU_W
LN_EPS = 1e-5
DEPTH = 2
DEEPNORM_ALPHA = (2.0 * DEPTH) ** 0.25
NEG_INF = -1e30
ATT_SCALE = HEAD_DIM ** -0.5

LANES = 128
SUBLANES = 8
T_CHUNK = 256
CONV_HALO = 32
LRU_HALO = SUBLANES
HALF = WINDOW // 2
KEYS = WINDOW + HALF
KPAD = 256
ROW_BLOCK = 32
PROJ_CHUNK = 512
STREAM_DTYPE = BF16
HEADS_PER_SCORE_TILE = 2
VMEM_LIMIT_MAIN = 60 * 1024 * 1024
VMEM_LIMIT_META = 48 * 1024 * 1024

def _silu(x):
    return x * jax.nn.sigmoid(x)


def _layer_norm(x, g, b):
    mu = jnp.mean(x, axis=-1, keepdims=True)
    xc = x - mu
    var = jnp.mean(xc * xc, axis=-1, keepdims=True)
    return xc * lax.rsqrt(var + LN_EPS) * g + b


def _bf16_rows(w_ref, r0, r1, c0, c1):
    return w_ref[r0:r1, c0:c1]


def _rotary(x, cos, s1, s2):
    return x * cos + pltpu.roll(x, LANES - ROT_DIM // 2, 1) * s1 + pltpu.roll(x, ROT_DIM // 2, 1) * s2


def _conv_rowblock(r, rb, pc_ref, cbuf_ref, dw_w_ref, dw_b_ref, g_ref, b_ref, cy_ref):
    accs = []
    for l in range(CONV_W // LANES):
        lo, hi = l * LANES, (l + 1) * LANES
        cbuf_ref[l, CONV_HALO + r:CONV_HALO + r + rb, :] = (
            pc_ref[r:r + rb, lo:hi] * jax.nn.sigmoid(pc_ref[r:r + rb, CONV_W + lo:CONV_W + hi]))
        acc = jnp.broadcast_to(dw_b_ref[:, lo:hi], (rb, LANES))
        for k in range(CONV_K):
            start = CONV_HALO - (CONV_K - 1) + k + r
            acc = acc + dw_w_ref[k:k + 1, lo:hi] * cbuf_ref[l, start:start + rb, :]
        accs.append(acc)
    y = _layer_norm(jnp.concatenate(accs, axis=1), g_ref[...], b_ref[...])
    cy_ref[r:r + rb, :] = _silu(y).astype(BF16)


def _conv_finish(rows, rb, pc_ref, cy_ref, pw_w_ref, pw_b_ref, ycat_ref):
    z = jnp.dot(cy_ref[...], _bf16_rows(pw_w_ref, 0, CONV_W, 0, CONV_W), preferred_element_type=F32) + pw_b_ref[...]
    for r in range(0, rows, rb):
        ycat_ref[r:r + rb, 0:CONV_W] = (z[r:r + rb] * _silu(pc_ref[r:r + rb, 2 * CONV_W:3 * CONV_W])).astype(BF16)


def _lru_conv_gates(rows, rb, pr_ref, rbuf_ref, xc_ref, gb_ref, lcw_ref, lcb_ref, wg_ref, bg_ref):
    for l in range(LRU_W // LANES):
        lo, hi = l * LANES, (l + 1) * LANES
        rbuf_ref[l, LRU_HALO:LRU_HALO + rows, :] = pr_ref[:, lo:hi]
        for r in range(0, rows, rb):
            acc = jnp.broadcast_to(lcb_ref[:, lo:hi], (rb, LANES))
            for k in range(LRU_CONV_K):
                start = LRU_HALO - (LRU_CONV_K - 1) + k + r
                acc = acc + lcw_ref[k:k + 1, lo:hi] * rbuf_ref[l, start:start + rb, :]
            xc_ref[r:r + rb, lo:hi] = acc
    gb_ref[...] = jnp.dot(xc_ref[...].astype(BF16), _bf16_rows(wg_ref, 0, LRU_W, 0, 2 * LRU_W),
                          preferred_element_type=F32) + bg_ref[...]


def _lru_group(r, carry, neg_c_softplus, pr_ref, xc_ref, gb_ref, ycat_ref, hout_ref=None):
    sub = lax.broadcasted_iota(jnp.int32, (SUBLANES, LRU_W), 0)
    x = xc_ref[r:r + SUBLANES, :]
    gate_r = jax.nn.sigmoid(gb_ref[r:r + SUBLANES, 0:LRU_W])
    gate_i = jax.nn.sigmoid(gb_ref[r:r + SUBLANES, LRU_W:2 * LRU_W])
    a = jnp.exp(gate_r * neg_c_softplus)
    u = jnp.sqrt(1.0 - a * a) * (gate_i * x)
    for d in (1, 2, 4):
        keep = sub >= d
        a_sh = jnp.where(keep, pltpu.roll(a, d, 0), 1.0)
        u_sh = jnp.where(keep, pltpu.roll(u, d, 0), 0.0)
        u = a * u_sh + u
        a = a * a_sh
    h = a * carry + u
    if hout_ref is not None:
        hout_ref[r:r + SUBLANES, :] = h
    ycat_ref[r:r + SUBLANES, CONV_W + ATT_W:OUT_IN] = (
        h * _silu(pr_ref[r:r + SUBLANES, LRU_W:2 * LRU_W])).astype(BF16)
    return jnp.broadcast_to(h[SUBLANES - 1:SUBLANES, :], (SUBLANES, LRU_W))


def _deepnorm_rows(resid, mixed, g_ref, b_ref):
    return _layer_norm(DEEPNORM_ALPHA * resid + mixed, g_ref[...], b_ref[...])


def _main_kernel(first_layer,
                 x_ref, cos_ref, s1_ref, s2_ref, lng_ref, lnb_ref, w_in_ref,
                 dw_w_ref, dw_b_ref, cg_ref, cb_ref, pw_w_ref, pw_b_ref, sink_ref,
                 lcw_ref, lcb_ref, wg_ref, bg_ref, lam_ref, w_out_ref, pg_ref, pb_ref,
                 cmeta_ref, kmeta_ref, vmeta_ref, rxmeta_ref, hlmeta_ref,
                 out_ref,
                 hb_ref, stat_ref, pc_ref, pa_ref, pr_ref, mix_ref, ycat_ref, cy_ref,
                 qbuf_ref, kbuf_ref, vbuf_ref, kmz_ref, vmz_ref, cbuf_ref, rbuf_ref, hcar_ref):
    T = T_CHUNK
    RB = ROW_BLOCK
    t = pl.program_id(1)

    @pl.when(t == 0)
    def _start_of_sequence():
        for l in range(CONV_W // LANES):
            cbuf_ref[l, 0:CONV_HALO - N_META, :] = jnp.zeros((CONV_HALO - N_META, LANES), F32)
            cbuf_ref[l, CONV_HALO - N_META:CONV_HALO, :] = cmeta_ref[:, l * LANES:(l + 1) * LANES]
            rbuf_ref[l, 0:LRU_HALO, :] = rxmeta_ref[N_META - LRU_HALO:N_META, l * LANES:(l + 1) * LANES]
        hcar_ref[...] = jnp.broadcast_to(hlmeta_ref[N_META - 1:N_META, :], (SUBLANES, LRU_W))
        kbuf_ref[:, 0:WINDOW, :] = jnp.zeros((N_KV, WINDOW, HEAD_DIM), BF16)
        vbuf_ref[:, 0:WINDOW, :] = jnp.zeros((N_KV, WINDOW, HEAD_DIM), BF16)
        pad = jnp.zeros((KPAD - KEYS - N_META, HEAD_DIM), BF16)
        for g in range(N_KV):
            kmz_ref[g, 0:N_META, :] = kmeta_ref[:, g * HEAD_DIM:(g + 1) * HEAD_DIM].astype(BF16)
            kmz_ref[g, N_META:KPAD - KEYS, :] = pad
            vmz_ref[g, 0:N_META, :] = vmeta_ref[:, g * HEAD_DIM:(g + 1) * HEAD_DIM].astype(BF16)
            vmz_ref[g, N_META:KPAD - KEYS, :] = pad

    def proj_chunk(dst_ref, c0, w0, n):
        dst_ref[:, c0:c0 + n] = jnp.dot(hb_ref[...], _bf16_rows(w_in_ref, 0, D_MODEL, w0, w0 + n),
                                        preferred_element_type=F32)

    def stream_rows(r):
        x = x_ref[0, r:r + RB, :]
        if not first_layer:
            return x.astype(F32)
        return (x - stat_ref[r:r + RB, 0:1]) * stat_ref[r:r + RB, 1:2] * lng_ref[...] + lnb_ref[...]

    for r in range(0, T, RB):
        if first_layer:
            x = x_ref[0, r:r + RB, :]
            mu = jnp.mean(x, axis=-1, keepdims=True)
            xc = x - mu
            stat_ref[r:r + RB, 0:1] = mu
            stat_ref[r:r + RB, 1:2] = lax.rsqrt(jnp.mean(xc * xc, axis=-1, keepdims=True) + LN_EPS)
        hb_ref[r:r + RB, :] = stream_rows(r).astype(BF16) if first_layer else x_ref[0, r:r + RB, :].astype(BF16)
    proj_chunk(pc_ref, 0, 0, COL_ATT)

    for r in range(0, T, RB):
        _conv_rowblock(r, RB, pc_ref, cbuf_ref, dw_w_ref, dw_b_ref, cg_ref, cb_ref, cy_ref)
    for l in range(CONV_W // LANES):
        cbuf_ref[l, 0:CONV_HALO, :] = cbuf_ref[l, T:T + CONV_HALO, :]
    for c in range(0, COL_LRU - COL_ATT, PROJ_CHUNK):
        proj_chunk(pa_ref, c, COL_ATT + c, PROJ_CHUNK)
    for c in range(0, IN_TOTAL - COL_LRU, PROJ_CHUNK):
        proj_chunk(pr_ref, c, COL_LRU + c, PROJ_CHUNK)
    _conv_finish(T, RB, pc_ref, cy_ref, pw_w_ref, pw_b_ref, ycat_ref)
    mix_ref[...] = jnp.dot(ycat_ref[:, 0:CONV_W], _bf16_rows(w_out_ref, 0, CONV_W, 0, D_MODEL),
                           preferred_element_type=F32)

    cos, s1, s2 = cos_ref[...], s1_ref[...], s2_ref[...]
    for j in range(ATT_W // LANES):
        rot = _rotary(pa_ref[:, j * LANES:(j + 1) * LANES], cos, s1, s2) * ATT_SCALE
        qbuf_ref[2 * j, :, :] = rot[:, 0:HEAD_DIM].astype(BF16)
        qbuf_ref[2 * j + 1, :, :] = rot[:, HEAD_DIM:LANES].astype(BF16)
    for j in range(KV_W // LANES):
        rot = _rotary(pa_ref[:, ATT_W + j * LANES:ATT_W + (j + 1) * LANES], cos, s1, s2)
        kbuf_ref[2 * j, WINDOW:WINDOW + T, :] = rot[:, 0:HEAD_DIM].astype(BF16)
        kbuf_ref[2 * j + 1, WINDOW:WINDOW + T, :] = rot[:, HEAD_DIM:LANES].astype(BF16)
        v = pa_ref[:, ATT_W + KV_W + j * LANES:ATT_W + KV_W + (j + 1) * LANES]
        vbuf_ref[2 * j, WINDOW:WINDOW + T, :] = v[:, 0:HEAD_DIM].astype(BF16)
        vbuf_ref[2 * j + 1, WINDOW:WINDOW + T, :] = v[:, HEAD_DIM:LANES].astype(BF16)

    gb_ref = pc_ref.at[:, 0:2 * LRU_W]
    xc_ref = pc_ref.at[:, 2 * LRU_W:2 * LRU_W + LRU_W]
    _lru_conv_gates(T, RB, pr_ref, rbuf_ref, xc_ref, gb_ref, lcw_ref, lcb_ref, wg_ref, bg_ref)
    for l in range(LRU_W // LANES):
        rbuf_ref[l, 0:LRU_HALO, :] = rbuf_ref[l, T:T + LRU_HALO, :]

    hp = HEADS_PER_SCORE_TILE
    nrow = hp * HALF
    qi = lax.broadcasted_iota(jnp.int32, (nrow, KPAD), 0) % HALF
    kc = lax.broadcasted_iota(jnp.int32, (nrow, KPAD), 1)
    visible = ((kc > qi) & (kc <= qi + WINDOW)) | ((kc >= KEYS) & (kc < KEYS + N_META))
    rowp = lax.broadcasted_iota(jnp.int32, (nrow, 1), 0) // HALF
    a_gate0 = ATT_W + 2 * KV_W

    def attn_sub_block(g, blk, s):
        r0 = blk * WINDOW + s * HALF
        kw = jnp.concatenate([kbuf_ref[g, r0:r0 + KEYS, :], kmz_ref[g]], axis=0)
        vw = jnp.concatenate([vbuf_ref[g, r0:r0 + KEYS, :], vmz_ref[g]], axis=0)
        for h0 in range(g * GROUP, (g + 1) * GROUP, hp):
            sink_col = jnp.zeros((nrow, 1), F32)
            for p in range(hp):
                sink_col = jnp.where(rowp == p, sink_ref[h0 + p], sink_col)
            q = jnp.concatenate([qbuf_ref[h0 + p, r0:r0 + HALF, :] for p in range(hp)], axis=0)
            sc = lax.dot_general(q, kw, (((1,), (1,)), ((), ())), preferred_element_type=F32)
            if blk == 0:
                first_key = jnp.where(t == 0, WINDOW - s * HALF, 0)
                sc = jnp.where(visible & (kc >= first_key), sc, NEG_INF)
            else:
                sc = jnp.where(visible, sc, NEG_INF)
            m = jnp.maximum(jnp.max(sc, axis=-1, keepdims=True), sink_col)
            pexp = jnp.exp(sc - m)
            den = jnp.sum(pexp, axis=-1, keepdims=True) + jnp.exp(sink_col - m)
            o = jnp.dot(pexp.astype(BF16), vw, preferred_element_type=F32) / den
            for p in range(hp):
                hcol = (h0 + p) * HEAD_DIM
                gate = _silu(pa_ref[r0:r0 + HALF, a_gate0 + hcol:a_gate0 + hcol + HEAD_DIM])
                ycat_ref[r0:r0 + HALF, CONV_W + hcol:CONV_W + hcol + HEAD_DIM] = (
                    o[p * HALF:(p + 1) * HALF, :] * gate).astype(BF16)

    sub_blocks = [(g, blk, s) for g in range(N_KV) for blk in range(T // WINDOW) for s in range(2)]
    groups_per_sub_block = 2 * (T // SUBLANES) // len(sub_blocks)
    neg_c_softplus = -LRU_C * jax.nn.softplus(-lam_ref[...])
    carry = hcar_ref[...]
    r_lru = 0
    for g, blk, s in sub_blocks:
        for _ in range(groups_per_sub_block):
            if r_lru < T:
                carry = _lru_group(r_lru, carry, neg_c_softplus, pr_ref, xc_ref, gb_ref, ycat_ref)
                r_lru += SUBLANES
        if r_lru == T:
            hcar_ref[...] = carry
            mix_ref[...] += jnp.dot(ycat_ref[:, CONV_W + ATT_W:OUT_IN],
                                    _bf16_rows(w_out_ref, CONV_W + ATT_W, OUT_IN, 0, D_MODEL),
                                    preferred_element_type=F32)
            r_lru += SUBLANES
        attn_sub_block(g, blk, s)
    assert r_lru == T + SUBLANES
    kbuf_ref[:, 0:WINDOW, :] = kbuf_ref[:, T:T + WINDOW, :]
    vbuf_ref[:, 0:WINDOW, :] = vbuf_ref[:, T:T + WINDOW, :]

    mix_ref[...] += jnp.dot(ycat_ref[:, CONV_W:CONV_W + ATT_W],
                            _bf16_rows(w_out_ref, CONV_W, CONV_W + ATT_W, 0, D_MODEL),
                            preferred_element_type=F32)
    for r in range(0, T, RB):
        out_ref[0, r:r + RB, :] = _deepnorm_rows(stream_rows(r), mix_ref[r:r + RB, :], pg_ref, pb_ref).astype(
            out_ref.dtype)


def _meta_kernel(first_layer,
                 x_ref, cos_ref, s1_ref, s2_ref, lng_ref, lnb_ref, w_in_ref,
                 dw_w_ref, dw_b_ref, cg_ref, cb_ref, pw_w_ref, pw_b_ref, sink_ref,
                 lcw_ref, lcb_ref, wg_ref, bg_ref, lam_ref, w_out_ref, pg_ref, pb_ref,
                 out_ref, cmeta_ref, kmeta_ref, vmeta_ref, rxmeta_ref, hlmeta_ref,
                 hb_ref, hres_ref, pc_ref, pa_ref, pr_ref, gb_ref, mixed_ref, ycat_ref, cy_ref,
                 cbuf_ref, rbuf_ref, xc_ref):
    M = N_META
    if first_layer:
        hres_ref[...] = _layer_norm(x_ref[...], lng_ref[...], lnb_ref[...])
    else:
        hres_ref[...] = x_ref[...]
    hb_ref[...] = hres_ref[...].astype(BF16)

    pc_ref[...] = jnp.dot(hb_ref[...], _bf16_rows(w_in_ref, 0, D_MODEL, 0, COL_ATT), preferred_element_type=F32)
    for l in range(CONV_W // LANES):
        cbuf_ref[l, 0:CONV_HALO, :] = jnp.zeros((CONV_HALO, LANES), F32)
    _conv_rowblock(0, M, pc_ref, cbuf_ref, dw_w_ref, dw_b_ref, cg_ref, cb_ref, cy_ref)
    _conv_finish(M, M, pc_ref, cy_ref, pw_w_ref, pw_b_ref, ycat_ref)
    for l in range(CONV_W // LANES):
        cmeta_ref[:, l * LANES:(l + 1) * LANES] = cbuf_ref[l, CONV_HALO:CONV_HALO + M, :]

    pa_ref[...] = jnp.dot(hb_ref[...], _bf16_rows(w_in_ref, 0, D_MODEL, COL_ATT, COL_LRU), preferred_element_type=F32)
    cos, s1, s2 = cos_ref[...], s1_ref[...], s2_ref[...]
    qs = [_rotary(pa_ref[:, j * LANES:(j + 1) * LANES], cos, s1, s2) * ATT_SCALE for j in range(ATT_W // LANES)]
    for j in range(KV_W // LANES):
        kmeta_ref[:, j * LANES:(j + 1) * LANES] = _rotary(
            pa_ref[:, ATT_W + j * LANES:ATT_W + (j + 1) * LANES], cos, s1, s2)
    vmeta_ref[...] = pa_ref[:, ATT_W + KV_W:ATT_W + 2 * KV_W]
    nrow = GROUP * M
    qi = lax.broadcasted_iota(jnp.int32, (nrow, M), 0) % M
    kc = lax.broadcasted_iota(jnp.int32, (nrow, M), 1)
    rowp = lax.broadcasted_iota(jnp.int32, (nrow, 1), 0) // M
    a_gate0 = ATT_W + 2 * KV_W
    for g in range(N_KV):
        sink_col = jnp.zeros((nrow, 1), F32)
        for p in range(GROUP):
            sink_col = jnp.where(rowp == p, sink_ref[g * GROUP + p], sink_col)
        heads = []
        for p in range(GROUP):
            hd = g * GROUP + p
            slab = qs[hd // 2]
            heads.append(slab[:, (hd % 2) * HEAD_DIM:(hd % 2 + 1) * HEAD_DIM].astype(BF16))
        q = jnp.concatenate(heads, axis=0)
        k = kmeta_ref[:, g * HEAD_DIM:(g + 1) * HEAD_DIM].astype(BF16)
        v = vmeta_ref[:, g * HEAD_DIM:(g + 1) * HEAD_DIM].astype(BF16)
        sc = lax.dot_general(q, k, (((1,), (1,)), ((), ())), preferred_element_type=F32)
        sc = jnp.where(kc <= qi, sc, NEG_INF)
        m = jnp.maximum(jnp.max(sc, axis=-1, keepdims=True), sink_col)
        pexp = jnp.exp(sc - m)
        den = jnp.sum(pexp, axis=-1, keepdims=True) + jnp.exp(sink_col - m)
        o = jnp.dot(pexp.astype(BF16), v, preferred_element_type=F32) / den
        for p in range(GROUP):
            hcol = (g * GROUP + p) * HEAD_DIM
            gate = _silu(pa_ref[:, a_gate0 + hcol:a_gate0 + hcol + HEAD_DIM])
            ycat_ref[:, CONV_W + hcol:CONV_W + hcol + HEAD_DIM] = (o[p * M:(p + 1) * M, :] * gate).astype(BF16)

    pr_ref[...] = jnp.dot(hb_ref[...], _bf16_rows(w_in_ref, 0, D_MODEL, COL_LRU, IN_TOTAL), preferred_element_type=F32)
    for l in range(LRU_W // LANES):
        rbuf_ref[l, 0:LRU_HALO, :] = jnp.zeros((LRU_HALO, LANES), F32)
    rxmeta_ref[...] = pr_ref[:, 0:LRU_W]
    _lru_conv_gates(M, M, pr_ref, rbuf_ref, xc_ref, gb_ref, lcw_ref, lcb_ref, wg_ref, bg_ref)
    neg_c_softplus = -LRU_C * jax.nn.softplus(-lam_ref[...])
    carry = jnp.zeros((SUBLANES, LRU_W), F32)
    for r in range(0, M, SUBLANES):
        carry = _lru_group(r, carry, neg_c_softplus, pr_ref, xc_ref, gb_ref, ycat_ref, hout_ref=hlmeta_ref)

    mixed_ref[...] = jnp.dot(ycat_ref[...], _bf16_rows(w_out_ref, 0, OUT_IN, 0, D_MODEL), preferred_element_type=F32)
    out_ref[...] = _deepnorm_rows(hres_ref[...], mixed_ref[...], pg_ref, pb_ref)


def _rope_tables(pos):
    half = ROT_DIM // 2
    inv_freq = ROPE_THETA ** (-jnp.arange(half, dtype=F32) / half)
    ang = pos.astype(F32)[:, None] * inv_freq[None, :]
    cos, sin = jnp.cos(ang), jnp.sin(ang)
    n = pos.shape[0]
    one = jnp.ones((n, HEAD_DIM - ROT_DIM), F32)
    zero = jnp.zeros((n, HEAD_DIM - ROT_DIM), F32)
    zh = jnp.zeros((n, half), F32)
    c64 = jnp.concatenate([cos, cos, one], axis=1)
    s1_64 = jnp.concatenate([-sin, zh, zero], axis=1)
    s2_64 = jnp.concatenate([zh, sin, zero], axis=1)
    rep = LANES // HEAD_DIM
    return jnp.tile(c64, (1, rep)), jnp.tile(s1_64, (1, rep)), jnp.tile(s2_64, (1, rep))


def _block_diag(w):
    h, d, _ = w.shape
    eye = jnp.eye(h, dtype=w.dtype)
    return (eye[:, None, :, None] * w[:, :, None, :]).reshape(h * d, h * d)


def _resident(shape):
    nd = len(shape)
    return pl.BlockSpec(shape, lambda *_: (0,) * nd, pipeline_mode=pl.Buffered(1))


def _layer_params(l, ln_in_g, ln_in_b, w_in, conv_dw_w, conv_dw_b, conv_ln_g, conv_ln_b, conv_pw_w, conv_pw_b,
                  attn_sinks, lru_conv_w, lru_conv_b, lru_wa, lru_ba, lru_wx, lru_bx, lru_lambda, w_out,
                  ln_post_g, ln_post_b):
    row = lambda a: a.reshape(1, -1)
    w_gate = jnp.concatenate([_block_diag(lru_wa[l]), _block_diag(lru_wx[l])], axis=-1).astype(BF16)
    b_gate = jnp.concatenate([lru_ba[l], lru_bx[l]], axis=-1)
    return [row(ln_in_g), row(ln_in_b), w_in[l].astype(BF16),
            conv_dw_w[l], row(conv_dw_b[l]), row(conv_ln_g[l]), row(conv_ln_b[l]),
            conv_pw_w[l].astype(BF16), row(conv_pw_b[l]),
            attn_sinks[l],
            lru_conv_w[l], row(lru_conv_b[l]), w_gate, row(b_gate), row(lru_lambda[l]),
            w_out[l].astype(BF16), row(ln_post_g[l]), row(ln_post_b[l])]


def _param_specs():
    return [_resident((1, D_MODEL)), _resident((1, D_MODEL)), _resident((D_MODEL, IN_TOTAL)),
            _resident((CONV_K, CONV_W)), _resident((1, CONV_W)), _resident((1, CONV_W)), _resident((1, CONV_W)),
            _resident((CONV_W, CONV_W)), _resident((1, CONV_W)),
            pl.BlockSpec(memory_space=pltpu.SMEM),
            _resident((LRU_CONV_K, LRU_W)), _resident((1, LRU_W)), _resident((LRU_W, 2 * LRU_W)),
            _resident((1, 2 * LRU_W)), _resident((1, LRU_W)),
            _resident((OUT_IN, D_MODEL)), _resident((1, D_MODEL)), _resident((1, D_MODEL))]


def _meta_call(first_layer, h_meta, tables, params):
    M = N_META
    f = lambda *shape: jax.ShapeDtypeStruct(shape, F32)
    out_shape = [f(M, D_MODEL), f(M, CONV_W), f(M, KV_W), f(M, KV_W), f(M, LRU_W), f(M, LRU_W)]
    in_specs = [_resident((M, D_MODEL))] + [_resident((M, LANES))] * 3 + _param_specs()
    out_specs = [pl.BlockSpec(s.shape, lambda i: (0, 0)) for s in out_shape]
    scratch = [pltpu.VMEM((M, D_MODEL), BF16), pltpu.VMEM((M, D_MODEL), F32),
               pltpu.VMEM((M, COL_ATT), F32), pltpu.VMEM((M, COL_LRU - COL_ATT), F32),
               pltpu.VMEM((M, 2 * LRU_W), F32), pltpu.VMEM((M, 2 * LRU_W), F32),
               pltpu.VMEM((M, D_MODEL), F32), pltpu.VMEM((M, OUT_IN), BF16), pltpu.VMEM((M, CONV_W), BF16),
               pltpu.VMEM((CONV_W // LANES, CONV_HALO + M, LANES), F32),
               pltpu.VMEM((LRU_W // LANES, LRU_HALO + M, LANES), F32),
               pltpu.VMEM((M, LRU_W), F32)]
    return pl.pallas_call(
        functools.partial(_meta_kernel, first_layer),
        grid=(1,), in_specs=in_specs, out_specs=out_specs, out_shape=out_shape, scratch_shapes=scratch,
        compiler_params=pltpu.CompilerParams(dimension_semantics=("arbitrary",), vmem_limit_bytes=VMEM_LIMIT_META),
        name="meta_layer1" if first_layer else "meta_layer2",
    )(h_meta, *tables, *params)


def _main_call(first_layer, h, tables, params, meta_state, out_dtype):
    B, S, _ = h.shape
    T = T_CHUNK
    M = N_META
    tok = pl.BlockSpec((1, T, D_MODEL), lambda b, t: (b, t, 0))
    tab = pl.BlockSpec((T, LANES), lambda b, t: (t, 0))
    in_specs = ([tok, tab, tab, tab] + _param_specs()
                + [_resident((M, CONV_W)), _resident((M, KV_W)), _resident((M, KV_W)),
                   _resident((M, LRU_W)), _resident((M, LRU_W))])
    scratch = [pltpu.VMEM((T, D_MODEL), BF16), pltpu.VMEM((T if first_layer else SUBLANES, LANES), F32),
               pltpu.VMEM((T, COL_ATT), F32), pltpu.VMEM((T, COL_LRU - COL_ATT), F32),
               pltpu.VMEM((T, 2 * LRU_W), F32),
               pltpu.VMEM((T, D_MODEL), F32), pltpu.VMEM((T, OUT_IN), BF16), pltpu.VMEM((T, CONV_W), BF16),
               pltpu.VMEM((N_HEADS, T, HEAD_DIM), BF16),
               pltpu.VMEM((N_KV, WINDOW + T, HEAD_DIM), BF16), pltpu.VMEM((N_KV, WINDOW + T, HEAD_DIM), BF16),
               pltpu.VMEM((N_KV, KPAD - KEYS, HEAD_DIM), BF16), pltpu.VMEM((N_KV, KPAD - KEYS, HEAD_DIM), BF16),
               pltpu.VMEM((CONV_W // LANES, CONV_HALO + T, LANES), F32),
               pltpu.VMEM((LRU_W // LANES, LRU_HALO + T, LANES), F32),
               pltpu.VMEM((SUBLANES, LRU_W), F32)]
    return pl.pallas_call(
        functools.partial(_main_kernel, first_layer),
        grid=(B, S // T), in_specs=in_specs, out_specs=tok,
        out_shape=jax.ShapeDtypeStruct((B, S, D_MODEL), out_dtype), scratch_shapes=scratch,
        compiler_params=pltpu.CompilerParams(dimension_semantics=("arbitrary", "arbitrary"),
                                             vmem_limit_bytes=VMEM_LIMIT_MAIN),
        name="tokens_layer1" if first_layer else "tokens_layer2",
    )(h, *tables, *params, *meta_state)


def kernel(x, meta_tokens, ln_in_g, ln_in_b, w_in, conv_dw_w, conv_dw_b, conv_ln_g, conv_ln_b, conv_pw_w, conv_pw_b,
           attn_sinks, lru_conv_w, lru_conv_b, lru_wa, lru_ba, lru_wx, lru_bx, lru_lambda, w_out, ln_post_g,
           ln_post_b):
    B, S, D = x.shape
    assert D == D_MODEL and S % T_CHUNK == 0 and w_in.shape == (DEPTH, D_MODEL, IN_TOTAL)
    meta_tables = _rope_tables(jnp.arange(N_META, dtype=jnp.int32))
    tok_tables = _rope_tables(N_META + jnp.arange(S, dtype=jnp.int32))

    h, h_meta = x, meta_tokens.astype(x.dtype)
    for l in range(DEPTH):
        params = _layer_params(l, ln_in_g, ln_in_b, w_in, conv_dw_w, conv_dw_b, conv_ln_g, conv_ln_b, conv_pw_w,
                               conv_pw_b, attn_sinks, lru_conv_w, lru_conv_b, lru_wa, lru_ba, lru_wx, lru_bx,
                               lru_lambda, w_out, ln_post_g, ln_post_b)
        h_meta, *meta_state = _meta_call(l == 0, h_meta, meta_tables, params)
        h = _main_call(l == 0, h, tok_tables, params, meta_state, x.dtype if l == DEPTH - 1 else STREAM_DTYPE)
    return h
```

```python
import functools

import jax
import jax.numpy as jnp
from jax import lax
from jax.experimental import pallas as pl
from jax.experimental.pallas import tpu as pltpu

F32 = jnp.float32
BF16 = jnp.bfloat16

D_MODEL = 2048
N_META = 16
CONV_W = 512
CONV_K = 31
HEAD_DIM = 64
N_HEADS = 16
N_KV = 4
GROUP = N_HEADS // N_KV
ATT_W = N_HEADS * HEAD_DIM
KV_W = N_KV * HEAD_DIM
WINDOW = 128
ROT_DIM = HEAD_DIM // 4
ROPE_THETA = 500000.0
LRU_W = 512
LRU_HEADS = 8
LRU_CONV_K = 4
LRU_C = 8.0
IN_TOTAL = 3 * CONV_W + 2 * ATT_W + 2 * KV_W + 2 * LRU_W
COL_ATT = 3 * CONV_W
COL_LRU = COL_ATT + 2 * ATT_W + 2 * KV_W
OUT_IN = CONV_W + ATT_W + LRU_W
LN_EPS = 1e-5
DEPTH = 2
DEEPNORM_ALPHA = (2.0 * DEPTH) ** 0.25
NEG_INF = -1e30
ATT_SCALE = HEAD_DIM ** -0.5

LANES = 128
SUBLANES = 8
T_CHUNK = 256
CONV_HALO = 32
LRU_HALO = SUBLANES
HALF = WINDOW // 2
KEYS = WINDOW + HALF
KPAD = 256
ROW_BLOCK = 32
PROJ_CHUNK = 512
HEADS_PER_SCORE_TILE = 2
VMEM_LIMIT_MAIN = 60 * 1024 * 1024
VMEM_LIMIT_META = 48 * 1024 * 1024


def _silu(x):
    return x * jax.nn.sigmoid(x)


def _layer_norm(x, g, b):
    mu = jnp.mean(x, axis=-1, keepdims=True)
    xc = x - mu
    var = jnp.mean(xc * xc, axis=-1, keepdims=True)
    return xc * lax.rsqrt(var + LN_EPS) * g + b


def _rotary(x, cos, s1, s2):
    return x * cos + pltpu.roll(x, LANES - ROT_DIM // 2, 1) * s1 + pltpu.roll(x, ROT_DIM // 2, 1) * s2


def _conv_rowblock(r, rb, pc_ref, cbuf_ref, dw_w_ref, dw_b_ref, g_ref, b_ref, cy_ref):
    accs = []
    for l in range(CONV_W // LANES):
        lo, hi = l * LANES, (l + 1) * LANES
        cbuf_ref[l, CONV_HALO + r:CONV_HALO + r + rb, :] = (
            pc_ref[r:r + rb, lo:hi] * jax.nn.sigmoid(pc_ref[r:r + rb, CONV_W + lo:CONV_W + hi]))
        acc = jnp.broadcast_to(dw_b_ref[:, lo:hi], (rb, LANES))
        for k in range(CONV_K):
            start = CONV_HALO - (CONV_K - 1) + k + r
            acc = acc + dw_w_ref[k:k + 1, lo:hi] * cbuf_ref[l, start:start + rb, :]
        accs.append(acc)
    y = _layer_norm(jnp.concatenate(accs, axis=1), g_ref[...], b_ref[...])
    cy_ref[r:r + rb, :] = _silu(y).astype(BF16)


def _conv_finish(rows, rb, pc_ref, cy_ref, pw_w_ref, pw_b_ref, ycat_ref):
    z = jnp.dot(cy_ref[...], pw_w_ref[...], preferred_element_type=F32) + pw_b_ref[...]
    for r in range(0, rows, rb):
        ycat_ref[r:r + rb, 0:CONV_W] = (z[r:r + rb] * _silu(pc_ref[r:r + rb, 2 * CONV_W:3 * CONV_W])).astype(BF16)


def _lru_conv_gates(rows, rb, pr_ref, rbuf_ref, xc_ref, gb_ref, lcw_ref, lcb_ref, wg_ref, bg_ref):
    for l in range(LRU_W // LANES):
        lo, hi = l * LANES, (l + 1) * LANES
        rbuf_ref[l, LRU_HALO:LRU_HALO + rows, :] = pr_ref[:, lo:hi]
        for r in range(0, rows, rb):
            acc = jnp.broadcast_to(lcb_ref[:, lo:hi], (rb, LANES))
            for k in range(LRU_CONV_K):
                start = LRU_HALO - (LRU_CONV_K - 1) + k + r
                acc = acc + lcw_ref[k:k + 1, lo:hi] * rbuf_ref[l, start:start + rb, :]
            xc_ref[r:r + rb, lo:hi] = acc
    gb_ref[...] = jnp.dot(xc_ref[...].astype(BF16), wg_ref[...], preferred_element_type=F32) + bg_ref[...]


def _lru_group(r, carry, neg_c_softplus, pr_ref, xc_ref, gb_ref, ycat_ref, hout_ref=None):
    sub = lax.broadcasted_iota(jnp.int32, (SUBLANES, LRU_W), 0)
    x = xc_ref[r:r + SUBLANES, :]
    gate_r = jax.nn.sigmoid(gb_ref[r:r + SUBLANES, 0:LRU_W])
    gate_i = jax.nn.sigmoid(gb_ref[r:r + SUBLANES, LRU_W:2 * LRU_W])
    a = jnp.exp(gate_r * neg_c_softplus)
    u = jnp.sqrt(1.0 - a * a) * (gate_i * x)
    for d in (1, 2, 4):
        keep = sub >= d
        a_sh = jnp.where(keep, pltpu.roll(a, d, 0), 1.0)
        u_sh = jnp.where(keep, pltpu.roll(u, d, 0), 0.0)
        u = a * u_sh + u
        a = a * a_sh
    h = a * carry + u
    if hout_ref is not None:
        hout_ref[r:r + SUBLANES, :] = h
    ycat_ref[r:r + SUBLANES, CONV_W + ATT_W:OUT_IN] = (
        h * _silu(pr_ref[r:r + SUBLANES, LRU_W:2 * LRU_W])).astype(BF16)
    return jnp.broadcast_to(h[SUBLANES - 1:SUBLANES, :], (SUBLANES, LRU_W))


def _deepnorm_rows(resid, mixed, g_ref, b_ref):
    return _layer_norm(DEEPNORM_ALPHA * resid + mixed, g_ref[...], b_ref[...])


def _main_kernel(layer,
                 x_ref, cos_ref, s1_ref, s2_ref, lng_ref, lnb_ref, w_in_ref,
                 dw_w_ref, dw_b_ref, cg_ref, cb_ref, pw_w_ref, pw_b_ref, sink_ref,
                 lcw_ref, lcb_ref, wg_ref, bg_ref, lam_ref, w_out_ref, pg_ref, pb_ref,
                 cmeta_ref, kmeta_ref, vmeta_ref, rxmeta_ref, hlmeta_ref,
                 out_ref,
                 hb_ref, hres_ref, pc_ref, pa_ref, pr_ref, mix_ref, ycat_ref, cy_ref,
                 qbuf_ref, kbuf_ref, vbuf_ref, kmz_ref, vmz_ref, cbuf_ref, rbuf_ref, hcar_ref):
    first_layer = layer == 0
    T = T_CHUNK
    RB = ROW_BLOCK
    t = pl.program_id(1)

    @pl.when(t == 0)
    def _start_of_sequence():
        for l in range(CONV_W // LANES):
            cbuf_ref[l, 0:CONV_HALO - N_META, :] = jnp.zeros((CONV_HALO - N_META, LANES), F32)
            cbuf_ref[l, CONV_HALO - N_META:CONV_HALO, :] = cmeta_ref[:, l * LANES:(l + 1) * LANES]
            rbuf_ref[l, 0:LRU_HALO, :] = rxmeta_ref[N_META - LRU_HALO:N_META, l * LANES:(l + 1) * LANES]
        hcar_ref[...] = jnp.broadcast_to(hlmeta_ref[N_META - 1:N_META, :], (SUBLANES, LRU_W))
        kbuf_ref[:, 0:WINDOW, :] = jnp.zeros((N_KV, WINDOW, HEAD_DIM), BF16)
        vbuf_ref[:, 0:WINDOW, :] = jnp.zeros((N_KV, WINDOW, HEAD_DIM), BF16)
        pad = jnp.zeros((KPAD - KEYS - N_META, HEAD_DIM), BF16)
        for g in range(N_KV):
            kmz_ref[g, 0:N_META, :] = kmeta_ref[:, g * HEAD_DIM:(g + 1) * HEAD_DIM].astype(BF16)
            kmz_ref[g, N_META:KPAD - KEYS, :] = pad
            vmz_ref[g, 0:N_META, :] = vmeta_ref[:, g * HEAD_DIM:(g + 1) * HEAD_DIM].astype(BF16)
            vmz_ref[g, N_META:KPAD - KEYS, :] = pad

    def proj_chunk(dst_ref, c0, w0, n):
        dst_ref[:, c0:c0 + n] = jnp.dot(hb_ref[...], w_in_ref[:, w0:w0 + n], preferred_element_type=F32)

    resid_ref = hres_ref if first_layer else x_ref.at[0]
    for r in range(0, T, RB):
        h = x_ref[0, r:r + RB, :]
        if first_layer:
            h = _layer_norm(h, lng_ref[...], lnb_ref[...])
            hres_ref[r:r + RB, :] = h
        hb_ref[r:r + RB, :] = h.astype(BF16)
    proj_chunk(pc_ref, 0, 0, COL_ATT)

    for r in range(0, T, RB):
        _conv_rowblock(r, RB, pc_ref, cbuf_ref, dw_w_ref, dw_b_ref, cg_ref, cb_ref, cy_ref)
    for l in range(CONV_W // LANES):
        cbuf_ref[l, 0:CONV_HALO, :] = cbuf_ref[l, T:T + CONV_HALO, :]
    for c in range(0, COL_LRU - COL_ATT, PROJ_CHUNK):
        proj_chunk(pa_ref, c, COL_ATT + c, PROJ_CHUNK)
    for c in range(0, IN_TOTAL - COL_LRU, PROJ_CHUNK):
        proj_chunk(pr_ref, c, COL_LRU + c, PROJ_CHUNK)
    _conv_finish(T, RB, pc_ref, cy_ref, pw_w_ref, pw_b_ref, ycat_ref)
    mix_ref[...] = jnp.dot(ycat_ref[:, 0:CONV_W], w_out_ref[0:CONV_W, :], preferred_element_type=F32)

    cos, s1, s2 = cos_ref[...], s1_ref[...], s2_ref[...]
    for j in range(ATT_W // LANES):
        rot = _rotary(pa_ref[:, j * LANES:(j + 1) * LANES], cos, s1, s2) * ATT_SCALE
        qbuf_ref[2 * j, :, :] = rot[:, 0:HEAD_DIM].astype(BF16)
        qbuf_ref[2 * j + 1, :, :] = rot[:, HEAD_DIM:LANES].astype(BF16)
    for j in range(KV_W // LANES):
        rot = _rotary(pa_ref[:, ATT_W + j * LANES:ATT_W + (j + 1) * LANES], cos, s1, s2)
        kbuf_ref[2 * j, WINDOW:WINDOW + T, :] = rot[:, 0:HEAD_DIM].astype(BF16)
        kbuf_ref[2 * j + 1, WINDOW:WINDOW + T, :] = rot[:, HEAD_DIM:LANES].astype(BF16)
        v = pa_ref[:, ATT_W + KV_W + j * LANES:ATT_W + KV_W + (j + 1) * LANES]
        vbuf_ref[2 * j, WINDOW:WINDOW + T, :] = v[:, 0:HEAD_DIM].astype(BF16)
        vbuf_ref[2 * j + 1, WINDOW:WINDOW + T, :] = v[:, HEAD_DIM:LANES].astype(BF16)

    gb_ref = pc_ref.at[:, 0:2 * LRU_W]
    xc_ref = pc_ref.at[:, 2 * LRU_W:2 * LRU_W + LRU_W]
    _lru_conv_gates(T, RB, pr_ref, rbuf_ref, xc_ref, gb_ref, lcw_ref, lcb_ref, wg_ref, bg_ref)
    for l in range(LRU_W // LANES):
        rbuf_ref[l, 0:LRU_HALO, :] = rbuf_ref[l, T:T + LRU_HALO, :]

    hp = HEADS_PER_SCORE_TILE
    nrow = hp * HALF
    qi = lax.broadcasted_iota(jnp.int32, (nrow, KPAD), 0) % HALF
    kc = lax.broadcasted_iota(jnp.int32, (nrow, KPAD), 1)
    visible = ((kc > qi) & (kc <= qi + WINDOW)) | ((kc >= KEYS) & (kc < KEYS + N_META))
    rowp = lax.broadcasted_iota(jnp.int32, (nrow, 1), 0) // HALF
    a_gate0 = ATT_W + 2 * KV_W

    def attn_sub_block(g, blk, s):
        r0 = blk * WINDOW + s * HALF
        kw = jnp.concatenate([kbuf_ref[g, r0:r0 + KEYS, :], kmz_ref[g]], axis=0)
        vw = jnp.concatenate([vbuf_ref[g, r0:r0 + KEYS, :], vmz_ref[g]], axis=0)
        for h0 in range(g * GROUP, (g + 1) * GROUP, hp):
            sink_col = jnp.zeros((nrow, 1), F32)
            for p in range(hp):
                sink_col = jnp.where(rowp == p, sink_ref[layer, h0 + p], sink_col)
            q = jnp.concatenate([qbuf_ref[h0 + p, r0:r0 + HALF, :] for p in range(hp)], axis=0)
            sc = lax.dot_general(q, kw, (((1,), (1,)), ((), ())), preferred_element_type=F32)
            if blk == 0:
                first_key = jnp.where(t == 0, WINDOW - s * HALF, 0)
                sc = jnp.where(visible & (kc >= first_key), sc, NEG_INF)
            else:
                sc = jnp.where(visible, sc, NEG_INF)
            m = jnp.maximum(jnp.max(sc, axis=-1, keepdims=True), sink_col)
            pexp = jnp.exp(sc - m)
            den = jnp.sum(pexp, axis=-1, keepdims=True) + jnp.exp(sink_col - m)
            o = jnp.dot(pexp.astype(BF16), vw, preferred_element_type=F32) / den
            for p in range(hp):
                hcol = (h0 + p) * HEAD_DIM
                gate = _silu(pa_ref[r0:r0 + HALF, a_gate0 + hcol:a_gate0 + hcol + HEAD_DIM])
                ycat_ref[r0:r0 + HALF, CONV_W + hcol:CONV_W + hcol + HEAD_DIM] = (
                    o[p * HALF:(p + 1) * HALF, :] * gate).astype(BF16)

    sub_blocks = [(g, blk, s) for g in range(N_KV) for blk in range(T // WINDOW) for s in range(2)]
    groups_per_sub_block = 2 * (T // SUBLANES) // len(sub_blocks)
    neg_c_softplus = -LRU_C * jax.nn.softplus(-lam_ref[...])
    carry = hcar_ref[...]
    r_lru = 0
    for g, blk, s in sub_blocks:
        for _ in range(groups_per_sub_block):
            if r_lru < T:
                carry = _lru_group(r_lru, carry, neg_c_softplus, pr_ref, xc_ref, gb_ref, ycat_ref)
                r_lru += SUBLANES
        if r_lru == T:
            hcar_ref[...] = carry
            mix_ref[...] += jnp.dot(ycat_ref[:, CONV_W + ATT_W:OUT_IN], w_out_ref[CONV_W + ATT_W:OUT_IN, :],
                                    preferred_element_type=F32)
            r_lru += SUBLANES
        attn_sub_block(g, blk, s)
    assert r_lru == T + SUBLANES
    kbuf_ref[:, 0:WINDOW, :] = kbuf_ref[:, T:T + WINDOW, :]
    vbuf_ref[:, 0:WINDOW, :] = vbuf_ref[:, T:T + WINDOW, :]

    mix_ref[...] += jnp.dot(ycat_ref[:, CONV_W:CONV_W + ATT_W], w_out_ref[CONV_W:CONV_W + ATT_W, :],
                            preferred_element_type=F32)
    for r in range(0, T, RB):
        out_ref[0, r:r + RB, :] = _deepnorm_rows(resid_ref[r:r + RB, :], mix_ref[r:r + RB, :], pg_ref, pb_ref)


def _meta_kernel(layer,
                 x_ref, cos_ref, s1_ref, s2_ref, lng_ref, lnb_ref, w_in_ref,
                 dw_w_ref, dw_b_ref, cg_ref, cb_ref, pw_w_ref, pw_b_ref, sink_ref,
                 lcw_ref, lcb_ref, wg_ref, bg_ref, lam_ref, w_out_ref, pg_ref, pb_ref,
                 out_ref, cmeta_ref, kmeta_ref, vmeta_ref, rxmeta_ref, hlmeta_ref,
                 hb_ref, hres_ref, pc_ref, pa_ref, pr_ref, gb_ref, mixed_ref, ycat_ref, cy_ref,
                 cbuf_ref, rbuf_ref, xc_ref):
    M = N_META
    if layer == 0:
        hres_ref[...] = _layer_norm(x_ref[...], lng_ref[...], lnb_ref[...])
    else:
        hres_ref[...] = x_ref[...]
    hb_ref[...] = hres_ref[...].astype(BF16)

    pc_ref[...] = jnp.dot(hb_ref[...], w_in_ref[:, 0:COL_ATT], preferred_element_type=F32)
    for l in range(CONV_W // LANES):
        cbuf_ref[l, 0:CONV_HALO, :] = jnp.zeros((CONV_HALO, LANES), F32)
    _conv_rowblock(0, M, pc_ref, cbuf_ref, dw_w_ref, dw_b_ref, cg_ref, cb_ref, cy_ref)
    _conv_finish(M, M, pc_ref, cy_ref, pw_w_ref, pw_b_ref, ycat_ref)
    for l in range(CONV_W // LANES):
        cmeta_ref[:, l * LANES:(l + 1) * LANES] = cbuf_ref[l, CONV_HALO:CONV_HALO + M, :]

    pa_ref[...] = jnp.dot(hb_ref[...], w_in_ref[:, COL_ATT:COL_LRU], preferred_element_type=F32)
    cos, s1, s2 = cos_ref[...], s1_ref[...], s2_ref[...]
    qs = [_rotary(pa_ref[:, j * LANES:(j + 1) * LANES], cos, s1, s2) * ATT_SCALE for j in range(ATT_W // LANES)]
    for j in range(KV_W // LANES):
        kmeta_ref[:, j * LANES:(j + 1) * LANES] = _rotary(
            pa_ref[:, ATT_W + j * LANES:ATT_W + (j + 1) * LANES], cos, s1, s2)
    vmeta_ref[...] = pa_ref[:, ATT_W + KV_W:ATT_W + 2 * KV_W]
    nrow = GROUP * M
    qi = lax.broadcasted_iota(jnp.int32, (nrow, M), 0) % M
    kc = lax.broadcasted_iota(jnp.int32, (nrow, M), 1)
    rowp = lax.broadcasted_iota(jnp.int32, (nrow, 1), 0) // M
    a_gate0 = ATT_W + 2 * KV_W
    for g in range(N_KV):
        sink_col = jnp.zeros((nrow, 1), F32)
        for p in range(GROUP):
            sink_col = jnp.where(rowp == p, sink_ref[layer, g * GROUP + p], sink_col)
        heads = []
        for p in range(GROUP):
            hd = g * GROUP + p
            slab = qs[hd // 2]
            heads.append(slab[:, (hd % 2) * HEAD_DIM:(hd % 2 + 1) * HEAD_DIM].astype(BF16))
        q = jnp.concatenate(heads, axis=0)
        k = kmeta_ref[:, g * HEAD_DIM:(g + 1) * HEAD_DIM].astype(BF16)
        v = vmeta_ref[:, g * HEAD_DIM:(g + 1) * HEAD_DIM].astype(BF16)
        sc = lax.dot_general(q, k, (((1,), (1,)), ((), ())), preferred_element_type=F32)
        sc = jnp.where(kc <= qi, sc, NEG_INF)
        m = jnp.maximum(jnp.max(sc, axis=-1, keepdims=True), sink_col)
        pexp = jnp.exp(sc - m)
        den = jnp.sum(pexp, axis=-1, keepdims=True) + jnp.exp(sink_col - m)
        o = jnp.dot(pexp.astype(BF16), v, preferred_element_type=F32) / den
        for p in range(GROUP):
            hcol = (g * GROUP + p) * HEAD_DIM
            gate = _silu(pa_ref[:, a_gate0 + hcol:a_gate0 + hcol + HEAD_DIM])
            ycat_ref[:, CONV_W + hcol:CONV_W + hcol + HEAD_DIM] = (o[p * M:(p + 1) * M, :] * gate).astype(BF16)

    pr_ref[...] = jnp.dot(hb_ref[...], w_in_ref[:, COL_LRU:IN_TOTAL], preferred_element_type=F32)
    for l in range(LRU_W // LANES):
        rbuf_ref[l, 0:LRU_HALO, :] = jnp.zeros((LRU_HALO, LANES), F32)
    rxmeta_ref[...] = pr_ref[:, 0:LRU_W]
    _lru_conv_gates(M, M, pr_ref, rbuf_ref, xc_ref, gb_ref, lcw_ref, lcb_ref, wg_ref, bg_ref)
    neg_c_softplus = -LRU_C * jax.nn.softplus(-lam_ref[...])
    carry = jnp.zeros((SUBLANES, LRU_W), F32)
    for r in range(0, M, SUBLANES):
        carry = _lru_group(r, carry, neg_c_softplus, pr_ref, xc_ref, gb_ref, ycat_ref, hout_ref=hlmeta_ref)

    mixed_ref[...] = jnp.dot(ycat_ref[...], w_out_ref[...], preferred_element_type=F32)
    out_ref[...] = _deepnorm_rows(hres_ref[...], mixed_ref[...], pg_ref, pb_ref)


def _rope_tables(pos):
    half = ROT_DIM // 2
    inv_freq = ROPE_THETA ** (-jnp.arange(half, dtype=F32) / half)
    ang = pos.astype(F32)[:, None] * inv_freq[None, :]
    cos, sin = jnp.cos(ang), jnp.sin(ang)
    n = pos.shape[0]
    one = jnp.ones((n, HEAD_DIM - ROT_DIM), F32)
    zero = jnp.zeros((n, HEAD_DIM - ROT_DIM), F32)
    zh = jnp.zeros((n, half), F32)
    c64 = jnp.concatenate([cos, cos, one], axis=1)
    s1_64 = jnp.concatenate([-sin, zh, zero], axis=1)
    s2_64 = jnp.concatenate([zh, sin, zero], axis=1)
    rep = LANES // HEAD_DIM
    return jnp.tile(c64, (1, rep)), jnp.tile(s1_64, (1, rep)), jnp.tile(s2_64, (1, rep))


def _block_diag(w):
    h, d, _ = w.shape
    eye = jnp.eye(h, dtype=w.dtype)
    return (eye[:, None, :, None] * w[:, :, None, :]).reshape(h * d, h * d)


def _resident(shape, layer=None):
    nd = len(shape)
    if layer is None:
        return pl.BlockSpec(shape, lambda *_: (0,) * nd, pipeline_mode=pl.Buffered(1))
    return pl.BlockSpec((None,) + shape, lambda *_: (layer,) + (0,) * nd, pipeline_mode=pl.Buffered(1))


def _stacked_params(ln_in_g, ln_in_b, w_in, conv_dw_w, conv_dw_b, conv_ln_g, conv_ln_b, conv_pw_w, conv_pw_b,
                    attn_sinks, lru_conv_w, lru_conv_b, lru_wa, lru_ba, lru_wx, lru_bx, lru_lambda, w_out,
                    ln_post_g, ln_post_b):
    rows = lambda a: a.reshape(a.shape[0], 1, -1)
    w_gate = jnp.concatenate([jax.vmap(_block_diag)(lru_wa), jax.vmap(_block_diag)(lru_wx)], axis=-1).astype(BF16)
    b_gate = jnp.concatenate([lru_ba, lru_bx], axis=-1)
    return [ln_in_g.reshape(1, -1), ln_in_b.reshape(1, -1), w_in.astype(BF16),
            conv_dw_w, rows(conv_dw_b), rows(conv_ln_g), rows(conv_ln_b), conv_pw_w.astype(BF16), rows(conv_pw_b),
            attn_sinks,
            lru_conv_w, rows(lru_conv_b), w_gate, rows(b_gate), rows(lru_lambda),
            w_out.astype(BF16), rows(ln_post_g), rows(ln_post_b)]


def _param_specs(l):
    return [_resident((1, D_MODEL)), _resident((1, D_MODEL)), _resident((D_MODEL, IN_TOTAL), l),
            _resident((CONV_K, CONV_W), l), _resident((1, CONV_W), l), _resident((1, CONV_W), l),
            _resident((1, CONV_W), l), _resident((CONV_W, CONV_W), l), _resident((1, CONV_W), l),
            pl.BlockSpec(memory_space=pltpu.SMEM),
            _resident((LRU_CONV_K, LRU_W), l), _resident((1, LRU_W), l), _resident((LRU_W, 2 * LRU_W), l),
            _resident((1, 2 * LRU_W), l), _resident((1, LRU_W), l),
            _resident((OUT_IN, D_MODEL), l), _resident((1, D_MODEL), l), _resident((1, D_MODEL), l)]


def _meta_call(layer, h_meta, tables, params):
    M = N_META
    f = lambda *shape: jax.ShapeDtypeStruct(shape, F32)
    out_shape = [f(M, D_MODEL), f(M, CONV_W), f(M, KV_W), f(M, KV_W), f(M, LRU_W), f(M, LRU_W)]
    in_specs = [_resident((M, D_MODEL))] + [_resident((M, LANES))] * 3 + _param_specs(layer)
    out_specs = [pl.BlockSpec(s.shape, lambda i: (0, 0)) for s in out_shape]
    scratch = [pltpu.VMEM((M, D_MODEL), BF16), pltpu.VMEM((M, D_MODEL), F32),
               pltpu.VMEM((M, COL_ATT), F32), pltpu.VMEM((M, COL_LRU - COL_ATT), F32),
               pltpu.VMEM((M, 2 * LRU_W), F32), pltpu.VMEM((M, 2 * LRU_W), F32),
               pltpu.VMEM((M, D_MODEL), F32), pltpu.VMEM((M, OUT_IN), BF16), pltpu.VMEM((M, CONV_W), BF16),
               pltpu.VMEM((CONV_W // LANES, CONV_HALO + M, LANES), F32),
               pltpu.VMEM((LRU_W // LANES, LRU_HALO + M, LANES), F32),
               pltpu.VMEM((M, LRU_W), F32)]
    return pl.pallas_call(
        functools.partial(_meta_kernel, layer),
        grid=(1,), in_specs=in_specs, out_specs=out_specs, out_shape=out_shape, scratch_shapes=scratch,
        compiler_params=pltpu.CompilerParams(dimension_semantics=("arbitrary",), vmem_limit_bytes=VMEM_LIMIT_META),
        name=f"meta_layer{layer + 1}",
    )(h_meta, *tables, *params)


def _main_call(layer, h, tables, params, meta_state):
    B, S, _ = h.shape
    T = T_CHUNK
    M = N_META
    tok = pl.BlockSpec((1, T, D_MODEL), lambda b, t: (b, t, 0))
    tab = pl.BlockSpec((T, LANES), lambda b, t: (t, 0))
    in_specs = ([tok, tab, tab, tab] + _param_specs(layer)
                + [_resident((M, CONV_W)), _resident((M, KV_W)), _resident((M, KV_W)),
                   _resident((M, LRU_W)), _resident((M, LRU_W))])
    scratch = [pltpu.VMEM((T, D_MODEL), BF16), pltpu.VMEM((T, D_MODEL) if layer == 0 else (SUBLANES, LANES), F32),
               pltpu.VMEM((T, COL_ATT), F32), pltpu.VMEM((T, COL_LRU - COL_ATT), F32),
               pltpu.VMEM((T, 2 * LRU_W), F32),
               pltpu.VMEM((T, D_MODEL), F32), pltpu.VMEM((T, OUT_IN), BF16), pltpu.VMEM((T, CONV_W), BF16),
               pltpu.VMEM((N_HEADS, T, HEAD_DIM), BF16),
               pltpu.VMEM((N_KV, WINDOW + T, HEAD_DIM), BF16), pltpu.VMEM((N_KV, WINDOW + T, HEAD_DIM), BF16),
               pltpu.VMEM((N_KV, KPAD - KEYS, HEAD_DIM), BF16), pltpu.VMEM((N_KV, KPAD - KEYS, HEAD_DIM), BF16),
               pltpu.VMEM((CONV_W // LANES, CONV_HALO + T, LANES), F32),
               pltpu.VMEM((LRU_W // LANES, LRU_HALO + T, LANES), F32),
               pltpu.VMEM((SUBLANES, LRU_W), F32)]
    return pl.pallas_call(
        functools.partial(_main_kernel, layer),
        grid=(B, S // T), in_specs=in_specs, out_specs=tok,
        out_shape=jax.ShapeDtypeStruct((B, S, D_MODEL), F32), scratch_shapes=scratch,
        compiler_params=pltpu.CompilerParams(dimension_semantics=("arbitrary", "arbitrary"),
                                             vmem_limit_bytes=VMEM_LIMIT_MAIN),
        name=f"tokens_layer{layer + 1}",
    )(h, *tables, *params, *meta_state)


def kernel(x, meta_tokens, ln_in_g, ln_in_b, w_in, conv_dw_w, conv_dw_b, conv_ln_g, conv_ln_b, conv_pw_w, conv_pw_b,
           attn_sinks, lru_conv_w, lru_conv_b, lru_wa, lru_ba, lru_wx, lru_bx, lru_lambda, w_out, ln_post_g,
           ln_post_b):
    B, S, D = x.shape
    assert D == D_MODEL and S % T_CHUNK == 0 and w_in.shape == (DEPTH, D_MODEL, IN_TOTAL)
    meta_tables = _rope_tables(jnp.arange(N_META, dtype=jnp.int32))
    tok_tables = _rope_tables(N_META + jnp.arange(S, dtype=jnp.int32))
    params = _stacked_params(ln_in_g, ln_in_b, w_in, conv_dw_w, conv_dw_b, conv_ln_g, conv_ln_b, conv_pw_w, conv_pw_b,
                             attn_sinks, lru_conv_w, lru_conv_b, lru_wa, lru_ba, lru_wx, lru_bx, lru_lambda, w_out,
                             ln_post_g, ln_post_b)
    h, h_meta = x, meta_tokens.astype(x.dtype)
    for l in range(DEPTH):
        h_meta, *meta_state = _meta_call(l, h_meta, meta_tables, params)
        h = _main_call(l, h, tok_tables, params, meta_state)
    return h
```

```python
import functools

import jax
import jax.numpy as jnp
from jax import lax
from jax.experimental import pallas as pl
from jax.experimental.pallas import tpu as pltpu

F32 = jnp.float32
BF16 = jnp.bfloat16

D_MODEL = 2048
N_META = 16
CONV_W = 512
CONV_K = 31
HEAD_DIM = 64
N_HEADS = 16
N_KV = 4
GROUP = N_HEADS // N_KV
ATT_W = N_HEADS * HEAD_DIM
KV_W = N_KV * HEAD_DIM
WINDOW = 128
ROT_DIM = HEAD_DIM // 4
ROPE_THETA = 500000.0
LRU_W = 512
LRU_HEADS = 8
LRU_CONV_K = 4
LRU_C = 8.0
IN_TOTAL = 3 * CONV_W + 2 * ATT_W + 2 * KV_W + 2 * LRU_W
COL_ATT = 3 * CONV_W
COL_LRU = COL_ATT + 2 * ATT_W + 2 * KV_W
OUT_IN = CONV_W + ATT_W + LRU_W
LN_EPS = 1e-5
DEPTH = 2
DEEPNORM_ALPHA = (2.0 * DEPTH) ** 0.25
NEG_INF = -1e30
ATT_SCALE = HEAD_DIM ** -0.5

LANES = 128
SUBLANES = 8
T_CHUNK = 256
CONV_HALO = 32
LRU_HALO = SUBLANES
HALF = WINDOW // 2
KEYS = WINDOW + HALF
KPAD = 256
ROW_BLOCK = 32
PROJ_CHUNK = 512
HEADS_PER_SCORE_TILE = 2
VMEM_LIMIT_MAIN = 60 * 1024 * 1024
VMEM_LIMIT_META = 48 * 1024 * 1024


VEC_ROWS = 16


class _Window:
    def __init__(self, ref, r0, r1, c0, c1):
        self.ref, self.r0, self.r1, self.c0, self.c1 = ref, r0, r1, c0, c1

    def __getitem__(self, idx):
        if idx is Ellipsis:
            return self.ref[self.r0:self.r1, self.c0:self.c1]
        rows, cols = idx
        ra, rb = rows.indices(self.r1 - self.r0)[:2]
        ca, cb = cols.indices(self.c1 - self.c0)[:2]
        return self.ref[self.r0 + ra:self.r0 + rb, self.c0 + ca:self.c0 + cb]


def _vector_views(vec_ref):
    w, q = D_MODEL, CONV_W
    win = functools.partial(_Window, vec_ref)
    return dict(
        lng=win(0, 1, 0, w), lnb=win(1, 2, 0, w), pg=win(2, 3, 0, w), pb=win(3, 4, 0, w),
        dw_b=win(4, 5, 0, q), cg=win(4, 5, q, 2 * q), cb=win(4, 5, 2 * q, 3 * q), pw_b=win(4, 5, 3 * q, w),
        lcb=win(5, 6, 0, q), lam=win(5, 6, q, 2 * q), bg=win(5, 6, 2 * q, w),
        lcw=win(8, 8 + LRU_CONV_K, 0, q))


def _silu(x):
    return x * jax.nn.sigmoid(x)


def _layer_norm(x, g, b):
    mu = jnp.mean(x, axis=-1, keepdims=True)
    xc = x - mu
    var = jnp.mean(xc * xc, axis=-1, keepdims=True)
    return xc * lax.rsqrt(var + LN_EPS) * g + b


def _rotary(x, cos, s1, s2):
    return x * cos + pltpu.roll(x, LANES - ROT_DIM // 2, 1) * s1 + pltpu.roll(x, ROT_DIM // 2, 1) * s2


def _conv_rowblock(r, rb, pc_ref, cbuf_ref, dw_w_ref, dw_b_ref, g_ref, b_ref, cy_ref):
    accs = []
    for l in range(CONV_W // LANES):
        lo, hi = l * LANES, (l + 1) * LANES
        cbuf_ref[l, CONV_HALO + r:CONV_HALO + r + rb, :] = (
            pc_ref[r:r + rb, lo:hi] * jax.nn.sigmoid(pc_ref[r:r + rb, CONV_W + lo:CONV_W + hi]))
        acc = jnp.broadcast_to(dw_b_ref[:, lo:hi], (rb, LANES))
        for k in range(CONV_K):
            start = CONV_HALO - (CONV_K - 1) + k + r
            acc = acc + dw_w_ref[k:k + 1, lo:hi] * cbuf_ref[l, start:start + rb, :]
        accs.append(acc)
    y = _layer_norm(jnp.concatenate(accs, axis=1), g_ref[...], b_ref[...])
    cy_ref[r:r + rb, :] = _silu(y).astype(BF16)


def _conv_finish(rows, rb, pc_ref, cy_ref, pw_w_ref, pw_b_ref, ycat_ref):
    z = jnp.dot(cy_ref[...], pw_w_ref[...], preferred_element_type=F32) + pw_b_ref[...]
    for r in range(0, rows, rb):
        ycat_ref[r:r + rb, 0:CONV_W] = (z[r:r + rb] * _silu(pc_ref[r:r + rb, 2 * CONV_W:3 * CONV_W])).astype(BF16)


def _lru_conv_gates(rows, rb, pr_ref, rbuf_ref, xc_ref, gb_ref, lcw_ref, lcb_ref, wg_ref, bg_ref):
    for l in range(LRU_W // LANES):
        lo, hi = l * LANES, (l + 1) * LANES
        rbuf_ref[l, LRU_HALO:LRU_HALO + rows, :] = pr_ref[:, lo:hi]
        for r in range(0, rows, rb):
            acc = jnp.broadcast_to(lcb_ref[:, lo:hi], (rb, LANES))
            for k in range(LRU_CONV_K):
                start = LRU_HALO - (LRU_CONV_K - 1) + k + r
                acc = acc + lcw_ref[k:k + 1, lo:hi] * rbuf_ref[l, start:start + rb, :]
            xc_ref[r:r + rb, lo:hi] = acc
    gb_ref[...] = jnp.dot(xc_ref[...].astype(BF16), wg_ref[...], preferred_element_type=F32) + bg_ref[...]


def _lru_group(r, carry, neg_c_softplus, pr_ref, xc_ref, gb_ref, ycat_ref, hout_ref=None):
    sub = lax.broadcasted_iota(jnp.int32, (SUBLANES, LRU_W), 0)
    x = xc_ref[r:r + SUBLANES, :]
    gate_r = jax.nn.sigmoid(gb_ref[r:r + SUBLANES, 0:LRU_W])
    gate_i = jax.nn.sigmoid(gb_ref[r:r + SUBLANES, LRU_W:2 * LRU_W])
    a = jnp.exp(gate_r * neg_c_softplus)
    u = jnp.sqrt(1.0 - a * a) * (gate_i * x)
    for d in (1, 2, 4):
        keep = sub >= d
        a_sh = jnp.where(keep, pltpu.roll(a, d, 0), 1.0)
        u_sh = jnp.where(keep, pltpu.roll(u, d, 0), 0.0)
        u = a * u_sh + u
        a = a * a_sh
    h = a * carry + u
    if hout_ref is not None:
        hout_ref[r:r + SUBLANES, :] = h
    ycat_ref[r:r + SUBLANES, CONV_W + ATT_W:OUT_IN] = (
        h * _silu(pr_ref[r:r + SUBLANES, LRU_W:2 * LRU_W])).astype(BF16)
    return jnp.broadcast_to(h[SUBLANES - 1:SUBLANES, :], (SUBLANES, LRU_W))


def _deepnorm_rows(resid, mixed, g_ref, b_ref):
    return _layer_norm(DEEPNORM_ALPHA * resid + mixed, g_ref[...], b_ref[...])


def _main_kernel(layer,
                 x_ref, cos_ref, s1_ref, s2_ref, vec_ref, w_in_ref, dw_w_ref, pw_w_ref, sink_ref, wg_ref, w_out_ref,
                 cmeta_ref, kmeta_ref, vmeta_ref, rxmeta_ref, hlmeta_ref,
                 out_ref,
                 hb_ref, hres_ref, pc_ref, pa_ref, pr_ref, mix_ref, ycat_ref, cy_ref,
                 qbuf_ref, kbuf_ref, vbuf_ref, kmz_ref, vmz_ref, cbuf_ref, rbuf_ref, hcar_ref):
    vv = _vector_views(vec_ref)
    lng_ref, lnb_ref, pg_ref, pb_ref = vv["lng"], vv["lnb"], vv["pg"], vv["pb"]
    dw_b_ref, cg_ref, cb_ref, pw_b_ref = vv["dw_b"], vv["cg"], vv["cb"], vv["pw_b"]
    lcw_ref, lcb_ref, bg_ref, lam_ref = vv["lcw"], vv["lcb"], vv["bg"], vv["lam"]
    first_layer = layer == 0
    T = T_CHUNK
    RB = ROW_BLOCK
    t = pl.program_id(1)

    @pl.when(t == 0)
    def _start_of_sequence():
        for l in range(CONV_W // LANES):
            cbuf_ref[l, 0:CONV_HALO - N_META, :] = jnp.zeros((CONV_HALO - N_META, LANES), F32)
            cbuf_ref[l, CONV_HALO - N_META:CONV_HALO, :] = cmeta_ref[:, l * LANES:(l + 1) * LANES]
            rbuf_ref[l, 0:LRU_HALO, :] = rxmeta_ref[N_META - LRU_HALO:N_META, l * LANES:(l + 1) * LANES]
        hcar_ref[...] = jnp.broadcast_to(hlmeta_ref[N_META - 1:N_META, :], (SUBLANES, LRU_W))
        kbuf_ref[:, 0:WINDOW, :] = jnp.zeros((N_KV, WINDOW, HEAD_DIM), BF16)
        vbuf_ref[:, 0:WINDOW, :] = jnp.zeros((N_KV, WINDOW, HEAD_DIM), BF16)
        pad = jnp.zeros((KPAD - KEYS - N_META, HEAD_DIM), BF16)
        for g in range(N_KV):
            kmz_ref[g, 0:N_META, :] = kmeta_ref[:, g * HEAD_DIM:(g + 1) * HEAD_DIM].astype(BF16)
            kmz_ref[g, N_META:KPAD - KEYS, :] = pad
            vmz_ref[g, 0:N_META, :] = vmeta_ref[:, g * HEAD_DIM:(g + 1) * HEAD_DIM].astype(BF16)
            vmz_ref[g, N_META:KPAD - KEYS, :] = pad

    def proj_chunk(dst_ref, c0, w0, n):
        dst_ref[:, c0:c0 + n] = jnp.dot(hb_ref[...], w_in_ref[:, w0:w0 + n], preferred_element_type=F32)

    resid_ref = hres_ref if first_layer else x_ref.at[0]
    for r in range(0, T, RB):
        h = x_ref[0, r:r + RB, :]
        if first_layer:
            h = _layer_norm(h, lng_ref[...], lnb_ref[...])
            hres_ref[r:r + RB, :] = h
        hb_ref[r:r + RB, :] = h.astype(BF16)
    proj_chunk(pc_ref, 0, 0, COL_ATT)

    for r in range(0, T, RB):
        _conv_rowblock(r, RB, pc_ref, cbuf_ref, dw_w_ref, dw_b_ref, cg_ref, cb_ref, cy_ref)
    for l in range(CONV_W // LANES):
        cbuf_ref[l, 0:CONV_HALO, :] = cbuf_ref[l, T:T + CONV_HALO, :]
    for c in range(0, COL_LRU - COL_ATT, PROJ_CHUNK):
        proj_chunk(pa_ref, c, COL_ATT + c, PROJ_CHUNK)
    for c in range(0, IN_TOTAL - COL_LRU, PROJ_CHUNK):
        proj_chunk(pr_ref, c, COL_LRU + c, PROJ_CHUNK)
    _conv_finish(T, RB, pc_ref, cy_ref, pw_w_ref, pw_b_ref, ycat_ref)
    mix_ref[...] = jnp.dot(ycat_ref[:, 0:CONV_W], w_out_ref[0:CONV_W, :], preferred_element_type=F32)

    cos, s1, s2 = cos_ref[...], s1_ref[...], s2_ref[...]
    for j in range(ATT_W // LANES):
        rot = _rotary(pa_ref[:, j * LANES:(j + 1) * LANES], cos, s1, s2) * ATT_SCALE
        qbuf_ref[2 * j, :, :] = rot[:, 0:HEAD_DIM].astype(BF16)
        qbuf_ref[2 * j + 1, :, :] = rot[:, HEAD_DIM:LANES].astype(BF16)
    for j in range(KV_W // LANES):
        rot = _rotary(pa_ref[:, ATT_W + j * LANES:ATT_W + (j + 1) * LANES], cos, s1, s2)
        kbuf_ref[2 * j, WINDOW:WINDOW + T, :] = rot[:, 0:HEAD_DIM].astype(BF16)
        kbuf_ref[2 * j + 1, WINDOW:WINDOW + T, :] = rot[:, HEAD_DIM:LANES].astype(BF16)
        v = pa_ref[:, ATT_W + KV_W + j * LANES:ATT_W + KV_W + (j + 1) * LANES]
        vbuf_ref[2 * j, WINDOW:WINDOW + T, :] = v[:, 0:HEAD_DIM].astype(BF16)
        vbuf_ref[2 * j + 1, WINDOW:WINDOW + T, :] = v[:, HEAD_DIM:LANES].astype(BF16)

    gb_ref = pc_ref.at[:, 0:2 * LRU_W]
    xc_ref = pc_ref.at[:, 2 * LRU_W:2 * LRU_W + LRU_W]
    _lru_conv_gates(T, RB, pr_ref, rbuf_ref, xc_ref, gb_ref, lcw_ref, lcb_ref, wg_ref, bg_ref)
    for l in range(LRU_W // LANES):
        rbuf_ref[l, 0:LRU_HALO, :] = rbuf_ref[l, T:T + LRU_HALO, :]

    hp = HEADS_PER_SCORE_TILE
    nrow = hp * HALF
    qi = lax.broadcasted_iota(jnp.int32, (nrow, KPAD), 0) % HALF
    kc = lax.broadcasted_iota(jnp.int32, (nrow, KPAD), 1)
    visible = ((kc > qi) & (kc <= qi + WINDOW)) | ((kc >= KEYS) & (kc < KEYS + N_META))
    rowp = lax.broadcasted_iota(jnp.int32, (nrow, 1), 0) // HALF
    a_gate0 = ATT_W + 2 * KV_W

    def attn_sub_block(g, blk, s):
        r0 = blk * WINDOW + s * HALF
        kw = jnp.concatenate([kbuf_ref[g, r0:r0 + KEYS, :], kmz_ref[g]], axis=0)
        vw = jnp.concatenate([vbuf_ref[g, r0:r0 + KEYS, :], vmz_ref[g]], axis=0)
        for h0 in range(g * GROUP, (g + 1) * GROUP, hp):
            sink_col = jnp.zeros((nrow, 1), F32)
            for p in range(hp):
                sink_col = jnp.where(rowp == p, sink_ref[layer, h0 + p], sink_col)
            q = jnp.concatenate([qbuf_ref[h0 + p, r0:r0 + HALF, :] for p in range(hp)], axis=0)
            sc = lax.dot_general(q, kw, (((1,), (1,)), ((), ())), preferred_element_type=F32)
            if blk == 0:
                first_key = jnp.where(t == 0, WINDOW - s * HALF, 0)
                sc = jnp.where(visible & (kc >= first_key), sc, NEG_INF)
            else:
                sc = jnp.where(visible, sc, NEG_INF)
            m = jnp.maximum(jnp.max(sc, axis=-1, keepdims=True), sink_col)
            pexp = jnp.exp(sc - m)
            den = jnp.sum(pexp, axis=-1, keepdims=True) + jnp.exp(sink_col - m)
            o = jnp.dot(pexp.astype(BF16), vw, preferred_element_type=F32) / den
            for p in range(hp):
                hcol = (h0 + p) * HEAD_DIM
                gate = _silu(pa_ref[r0:r0 + HALF, a_gate0 + hcol:a_gate0 + hcol + HEAD_DIM])
                ycat_ref[r0:r0 + HALF, CONV_W + hcol:CONV_W + hcol + HEAD_DIM] = (
                    o[p * HALF:(p + 1) * HALF, :] * gate).astype(BF16)

    sub_blocks = [(g, blk, s) for g in range(N_KV) for blk in range(T // WINDOW) for s in range(2)]
    groups_per_sub_block = 2 * (T // SUBLANES) // len(sub_blocks)
    neg_c_softplus = -LRU_C * jax.nn.softplus(-lam_ref[...])
    carry = hcar_ref[...]
    r_lru = 0
    for g, blk, s in sub_blocks:
        for _ in range(groups_per_sub_block):
            if r_lru < T:
                carry = _lru_group(r_lru, carry, neg_c_softplus, pr_ref, xc_ref, gb_ref, ycat_ref)
                r_lru += SUBLANES
        if r_lru == T:
            hcar_ref[...] = carry
            mix_ref[...] += jnp.dot(ycat_ref[:, CONV_W + ATT_W:OUT_IN], w_out_ref[CONV_W + ATT_W:OUT_IN, :],
                                    preferred_element_type=F32)
            r_lru += SUBLANES
        attn_sub_block(g, blk, s)
    assert r_lru == T + SUBLANES
    kbuf_ref[:, 0:WINDOW, :] = kbuf_ref[:, T:T + WINDOW, :]
    vbuf_ref[:, 0:WINDOW, :] = vbuf_ref[:, T:T + WINDOW, :]

    mix_ref[...] += jnp.dot(ycat_ref[:, CONV_W:CONV_W + ATT_W], w_out_ref[CONV_W:CONV_W + ATT_W, :],
                            preferred_element_type=F32)
    for r in range(0, T, RB):
        out_ref[0, r:r + RB, :] = _deepnorm_rows(resid_ref[r:r + RB, :], mix_ref[r:r + RB, :], pg_ref, pb_ref)


def _meta_kernel(layer,
                 x_ref, cos_ref, s1_ref, s2_ref, vec_ref, w_in_ref, dw_w_ref, pw_w_ref, sink_ref, wg_ref, w_out_ref,
                 out_ref, cmeta_ref, kmeta_ref, vmeta_ref, rxmeta_ref, hlmeta_ref,
                 hb_ref, hres_ref, pc_ref, pa_ref, pr_ref, gb_ref, mixed_ref, ycat_ref, cy_ref,
                 cbuf_ref, rbuf_ref, xc_ref):
    vv = _vector_views(vec_ref)
    lng_ref, lnb_ref, pg_ref, pb_ref = vv["lng"], vv["lnb"], vv["pg"], vv["pb"]
    dw_b_ref, cg_ref, cb_ref, pw_b_ref = vv["dw_b"], vv["cg"], vv["cb"], vv["pw_b"]
    lcw_ref, lcb_ref, bg_ref, lam_ref = vv["lcw"], vv["lcb"], vv["bg"], vv["lam"]
    M = N_META
    if layer == 0:
        hres_ref[...] = _layer_norm(x_ref[...], lng_ref[...], lnb_ref[...])
    else:
        hres_ref[...] = x_ref[...]
    hb_ref[...] = hres_ref[...].astype(BF16)

    pc_ref[...] = jnp.dot(hb_ref[...], w_in_ref[:, 0:COL_ATT], preferred_element_type=F32)
    for l in range(CONV_W // LANES):
        cbuf_ref[l, 0:CONV_HALO, :] = jnp.zeros((CONV_HALO, LANES), F32)
    _conv_rowblock(0, M, pc_ref, cbuf_ref, dw_w_ref, dw_b_ref, cg_ref, cb_ref, cy_ref)
    _conv_finish(M, M, pc_ref, cy_ref, pw_w_ref, pw_b_ref, ycat_ref)
    for l in range(CONV_W // LANES):
        cmeta_ref[:, l * LANES:(l + 1) * LANES] = cbuf_ref[l, CONV_HALO:CONV_HALO + M, :]

    pa_ref[...] = jnp.dot(hb_ref[...], w_in_ref[:, COL_ATT:COL_LRU], preferred_element_type=F32)
    cos, s1, s2 = cos_ref[...], s1_ref[...], s2_ref[...]
    qs = [_rotary(pa_ref[:, j * LANES:(j + 1) * LANES], cos, s1, s2) * ATT_SCALE for j in range(ATT_W // LANES)]
    for j in range(KV_W // LANES):
        kmeta_ref[:, j * LANES:(j + 1) * LANES] = _rotary(
            pa_ref[:, ATT_W + j * LANES:ATT_W + (j + 1) * LANES], cos, s1, s2)
    vmeta_ref[...] = pa_ref[:, ATT_W + KV_W:ATT_W + 2 * KV_W]
    nrow = GROUP * M
    qi = lax.broadcasted_iota(jnp.int32, (nrow, M), 0) % M
    kc = lax.broadcasted_iota(jnp.int32, (nrow, M), 1)
    rowp = lax.broadcasted_iota(jnp.int32, (nrow, 1), 0) // M
    a_gate0 = ATT_W + 2 * KV_W
    for g in range(N_KV):
        sink_col = jnp.zeros((nrow, 1), F32)
        for p in range(GROUP):
            sink_col = jnp.where(rowp == p, sink_ref[layer, g * GROUP + p], sink_col)
        heads = []
        for p in range(GROUP):
            hd = g * GROUP + p
            slab = qs[hd // 2]
            heads.append(slab[:, (hd % 2) * HEAD_DIM:(hd % 2 + 1) * HEAD_DIM].astype(BF16))
        q = jnp.concatenate(heads, axis=0)
        k = kmeta_ref[:, g * HEAD_DIM:(g + 1) * HEAD_DIM].astype(BF16)
        v = vmeta_ref[:, g * HEAD_DIM:(g + 1) * HEAD_DIM].astype(BF16)
        sc = lax.dot_general(q, k, (((1,), (1,)), ((), ())), preferred_element_type=F32)
        sc = jnp.where(kc <= qi, sc, NEG_INF)
        m = jnp.maximum(jnp.max(sc, axis=-1, keepdims=True), sink_col)
        pexp = jnp.exp(sc - m)
        den = jnp.sum(pexp, axis=-1, keepdims=True) + jnp.exp(sink_col - m)
        o = jnp.dot(pexp.astype(BF16), v, preferred_element_type=F32) / den
        for p in range(GROUP):
            hcol = (g * GROUP + p) * HEAD_DIM
            gate = _silu(pa_ref[:, a_gate0 + hcol:a_gate0 + hcol + HEAD_DIM])
            ycat_ref[:, CONV_W + hcol:CONV_W + hcol + HEAD_DIM] = (o[p * M:(p + 1) * M, :] * gate).astype(BF16)

    pr_ref[...] = jnp.dot(hb_ref[...], w_in_ref[:, COL_LRU:IN_TOTAL], preferred_element_type=F32)
    for l in range(LRU_W // LANES):
        rbuf_ref[l, 0:LRU_HALO, :] = jnp.zeros((LRU_HALO, LANES), F32)
    rxmeta_ref[...] = pr_ref[:, 0:LRU_W]
    _lru_conv_gates(M, M, pr_ref, rbuf_ref, xc_ref, gb_ref, lcw_ref, lcb_ref, wg_ref, bg_ref)
    neg_c_softplus = -LRU_C * jax.nn.softplus(-lam_ref[...])
    carry = jnp.zeros((SUBLANES, LRU_W), F32)
    for r in range(0, M, SUBLANES):
        carry = _lru_group(r, carry, neg_c_softplus, pr_ref, xc_ref, gb_ref, ycat_ref, hout_ref=hlmeta_ref)

    mixed_ref[...] = jnp.dot(ycat_ref[...], w_out_ref[...], preferred_element_type=F32)
    out_ref[...] = _deepnorm_rows(hres_ref[...], mixed_ref[...], pg_ref, pb_ref)


def _rope_tables(pos):
    half = ROT_DIM // 2
    inv_freq = ROPE_THETA ** (-jnp.arange(half, dtype=F32) / half)
    ang = pos.astype(F32)[:, None] * inv_freq[None, :]
    cos, sin = jnp.cos(ang), jnp.sin(ang)
    n = pos.shape[0]
    one = jnp.ones((n, HEAD_DIM - ROT_DIM), F32)
    zero = jnp.zeros((n, HEAD_DIM - ROT_DIM), F32)
    zh = jnp.zeros((n, half), F32)
    c64 = jnp.concatenate([cos, cos, one], axis=1)
    s1_64 = jnp.concatenate([-sin, zh, zero], axis=1)
    s2_64 = jnp.concatenate([zh, sin, zero], axis=1)
    rep = LANES // HEAD_DIM
    return jnp.tile(c64, (1, rep)), jnp.tile(s1_64, (1, rep)), jnp.tile(s2_64, (1, rep))


def _block_diag(w):
    h, d, _ = w.shape
    eye = jnp.eye(h, dtype=w.dtype)
    return (eye[:, None, :, None] * w[:, :, None, :]).reshape(h * d, h * d)


def _resident(shape, layer=None):
    nd = len(shape)
    if layer is None:
        return pl.BlockSpec(shape, lambda *_: (0,) * nd, pipeline_mode=pl.Buffered(1))
    return pl.BlockSpec((None,) + shape, lambda *_: (layer,) + (0,) * nd, pipeline_mode=pl.Buffered(1))


def _pack_vectors(ln_in_g, ln_in_b, conv_dw_b, conv_ln_g, conv_ln_b, conv_pw_b, lru_conv_w, lru_conv_b, lru_ba,
                  lru_bx, lru_lambda, ln_post_g, ln_post_b):
    depth = ln_post_g.shape[0]
    shared = lambda a: jnp.broadcast_to(a.reshape(1, 1, -1), (depth, 1, D_MODEL))
    row = lambda *parts: jnp.concatenate(parts, axis=-1)[:, None, :]
    zeros = lambda n: jnp.zeros((depth, n, D_MODEL), F32)
    taps = jnp.pad(lru_conv_w, ((0, 0), (0, 0), (0, D_MODEL - LRU_W)))
    return jnp.concatenate(
        [shared(ln_in_g), shared(ln_in_b), row(ln_post_g), row(ln_post_b),
         row(conv_dw_b, conv_ln_g, conv_ln_b, conv_pw_b), row(lru_conv_b, lru_lambda, lru_ba, lru_bx),
         zeros(2), taps, zeros(VEC_ROWS - 8 - LRU_CONV_K)], axis=1)


def _stacked_params(ln_in_g, ln_in_b, w_in, conv_dw_w, conv_dw_b, conv_ln_g, conv_ln_b, conv_pw_w, conv_pw_b,
                    attn_sinks, lru_conv_w, lru_conv_b, lru_wa, lru_ba, lru_wx, lru_bx, lru_lambda, w_out,
                    ln_post_g, ln_post_b):
    vecs = _pack_vectors(ln_in_g, ln_in_b, conv_dw_b, conv_ln_g, conv_ln_b, conv_pw_b, lru_conv_w, lru_conv_b,
                         lru_ba, lru_bx, lru_lambda, ln_post_g, ln_post_b)
    w_gate = jnp.concatenate([jax.vmap(_block_diag)(lru_wa), jax.vmap(_block_diag)(lru_wx)], axis=-1).astype(BF16)
    return [vecs, w_in.astype(BF16), conv_dw_w, conv_pw_w.astype(BF16), attn_sinks, w_gate, w_out.astype(BF16)]


def _param_specs(l):
    return [_resident((VEC_ROWS, D_MODEL), l), _resident((D_MODEL, IN_TOTAL), l), _resident((CONV_K, CONV_W), l),
            _resident((CONV_W, CONV_W), l), pl.BlockSpec(memory_space=pltpu.SMEM),
            _resident((LRU_W, 2 * LRU_W), l), _resident((OUT_IN, D_MODEL), l)]


def _meta_call(layer, h_meta, tables, params):
    M = N_META
    f = lambda *shape: jax.ShapeDtypeStruct(shape, F32)
    out_shape = [f(M, D_MODEL), f(M, CONV_W), f(M, KV_W), f(M, KV_W), f(M, LRU_W), f(M, LRU_W)]
    in_specs = [_resident((M, D_MODEL))] + [_resident((M, LANES))] * 3 + _param_specs(layer)
    out_specs = [pl.BlockSpec(s.shape, lambda i: (0, 0)) for s in out_shape]
    scratch = [pltpu.VMEM((M, D_MODEL), BF16), pltpu.VMEM((M, D_MODEL), F32),
               pltpu.VMEM((M, COL_ATT), F32), pltpu.VMEM((M, COL_LRU - COL_ATT), F32),
               pltpu.VMEM((M, 2 * LRU_W), F32), pltpu.VMEM((M, 2 * LRU_W), F32),
               pltpu.VMEM((M, D_MODEL), F32), pltpu.VMEM((M, OUT_IN), BF16), pltpu.VMEM((M, CONV_W), BF16),
               pltpu.VMEM((CONV_W // LANES, CONV_HALO + M, LANES), F32),
               pltpu.VMEM((LRU_W // LANES, LRU_HALO + M, LANES), F32),
               pltpu.VMEM((M, LRU_W), F32)]
    return pl.pallas_call(
        functools.partial(_meta_kernel, layer),
        grid=(1,), in_specs=in_specs, out_specs=out_specs, out_shape=out_shape, scratch_shapes=scratch,
        compiler_params=pltpu.CompilerParams(dimension_semantics=("arbitrary",), vmem_limit_bytes=VMEM_LIMIT_META),
        name=f"meta_layer{layer + 1}",
    )(h_meta, *tables, *params)


def _main_call(layer, h, tables, params, meta_state):
    B, S, _ = h.shape
    T = T_CHUNK
    M = N_META
    tok = pl.BlockSpec((1, T, D_MODEL), lambda b, t: (b, t, 0))
    tab = pl.BlockSpec((T, LANES), lambda b, t: (t, 0))
    in_specs = ([tok, tab, tab, tab] + _param_specs(layer)
                + [_resident((M, CONV_W)), _resident((M, KV_W)), _resident((M, KV_W)),
                   _resident((M, LRU_W)), _resident((M, LRU_W))])
    scratch = [pltpu.VMEM((T, D_MODEL), BF16), pltpu.VMEM((T, D_MODEL) if layer == 0 else (SUBLANES, LANES), F32),
               pltpu.VMEM((T, COL_ATT), F32), pltpu.VMEM((T, COL_LRU - COL_ATT), F32),
               pltpu.VMEM((T, 2 * LRU_W), F32),
               pltpu.VMEM((T, D_MODEL), F32), pltpu.VMEM((T, OUT_IN), BF16), pltpu.VMEM((T, CONV_W), BF16),
               pltpu.VMEM((N_HEADS, T, HEAD_DIM), BF16),
               pltpu.VMEM((N_KV, WINDOW + T, HEAD_DIM), BF16), pltpu.VMEM((N_KV, WINDOW + T, HEAD_DIM), BF16),
               pltpu.VMEM((N_KV, KPAD - KEYS, HEAD_DIM), BF16), pltpu.VMEM((N_KV, KPAD - KEYS, HEAD_DIM), BF16),
               pltpu.VMEM((CONV_W // LANES, CONV_HALO + T, LANES), F32),
               pltpu.VMEM((LRU_W // LANES, LRU_HALO + T, LANES), F32),
               pltpu.VMEM((SUBLANES, LRU_W), F32)]
    return pl.pallas_call(
        functools.partial(_main_kernel, layer),
        grid=(B, S // T), in_specs=in_specs, out_specs=tok,
        out_shape=jax.ShapeDtypeStruct((B, S, D_MODEL), F32), scratch_shapes=scratch,
        compiler_params=pltpu.CompilerParams(dimension_semantics=("arbitrary", "arbitrary"),
                                             vmem_limit_bytes=VMEM_LIMIT_MAIN),
        name=f"tokens_layer{layer + 1}",
    )(h, *tables, *params, *meta_state)


def kernel(x, meta_tokens, ln_in_g, ln_in_b, w_in, conv_dw_w, conv_dw_b, conv_ln_g, conv_ln_b, conv_pw_w, conv_pw_b,
           attn_sinks, lru_conv_w, lru_conv_b, lru_wa, lru_ba, lru_wx, lru_bx, lru_lambda, w_out, ln_post_g,
           ln_post_b):
    B, S, D = x.shape
    assert D == D_MODEL and S % T_CHUNK == 0 and w_in.shape == (DEPTH, D_MODEL, IN_TOTAL)
    meta_tables = _rope_tables(jnp.arange(N_META, dtype=jnp.int32))
    tok_tables = _rope_tables(N_META + jnp.arange(S, dtype=jnp.int32))
    params = _stacked_params(ln_in_g, ln_in_b, w_in, conv_dw_w, conv_dw_b, conv_ln_g, conv_ln_b, conv_pw_w, conv_pw_b,
                             attn_sinks, lru_conv_w, lru_conv_b, lru_wa, lru_ba, lru_wx, lru_bx, lru_lambda, w_out,
                             ln_post_g, ln_post_b)
    h, h_meta = x, meta_tokens.astype(x.dtype)
    for l in range(DEPTH):
        h_meta, *meta_state = _meta_call(l, h_meta, meta_tables, params)
        h = _main_call(l, h, tok_tables, params, meta_state)
    return h
```

```python
import functools

import jax
import jax.numpy as jnp
from jax import lax
from jax.experimental import pallas as pl
from jax.experimental.pallas import tpu as pltpu

F32 = jnp.float32
BF16 = jnp.bfloat16

D_MODEL = 2048
N_META = 16
CONV_W = 512
CONV_K = 31
HEAD_DIM = 64
N_HEADS = 16
N_KV = 4
GROUP = N_HEADS // N_KV
ATT_W = N_HEADS * HEAD_DIM
KV_W = N_KV * HEAD_DIM
WINDOW = 128
ROT_DIM = HEAD_DIM // 4
ROPE_THETA = 500000.0
LRU_W = 512
LRU_HEADS = 8
LRU_CONV_K = 4
LRU_C = 8.0
IN_TOTAL = 3 * CONV_W + 2 * ATT_W + 2 * KV_W + 2 * LRU_W
COL_ATT = 3 * CONV_W
COL_LRU = COL_ATT + 2 * ATT_W + 2 * KV_W
OUT_IN = CONV_W + ATT_W + LRU_W
LN_EPS = 1e-5
DEPTH = 2
DEEPNORM_ALPHA = (2.0 * DEPTH) ** 0.25
NEG_INF = -1e30
ATT_SCALE = HEAD_DIM ** -0.5

LANES = 128
SUBLANES = 8
T_CHUNK = 256
CONV_HALO = 32
LRU_HALO = SUBLANES
HALF = WINDOW // 2
KEYS = WINDOW + HALF
KPAD = 256
SLAB_SKEW = 8
ROW_BLOCK = 32
PROJ_CHUNK = 512
HEADS_PER_SCORE_TILE = 2
VMEM_LIMIT_MAIN = 60 * 1024 * 1024
VMEM_LIMIT_META = 48 * 1024 * 1024


def _silu(x):
    return x * jax.nn.sigmoid(x)


def _layer_norm(x, g, b):
    mu = jnp.mean(x, axis=-1, keepdims=True)
    xc = x - mu
    var = jnp.mean(xc * xc, axis=-1, keepdims=True)
    return xc * lax.rsqrt(var + LN_EPS) * g + b


def _rotary(x, cos, s1, s2):
    return x * cos + pltpu.roll(x, LANES - ROT_DIM // 2, 1) * s1 + pltpu.roll(x, ROT_DIM // 2, 1) * s2


def _conv_rowblock(r, rb, pc_ref, cbuf_ref, dw_w_ref, dw_b_ref, g_ref, b_ref, cy_ref):
    accs = []
    for l in range(CONV_W // LANES):
        lo, hi = l * LANES, (l + 1) * LANES
        cbuf_ref[l, CONV_HALO + r:CONV_HALO + r + rb, :] = (
            pc_ref[r:r + rb, lo:hi] * jax.nn.sigmoid(pc_ref[r:r + rb, CONV_W + lo:CONV_W + hi]))
        acc = jnp.broadcast_to(dw_b_ref[:, lo:hi], (rb, LANES))
        for k in range(CONV_K):
            start = CONV_HALO - (CONV_K - 1) + k + r
            acc = acc + dw_w_ref[k:k + 1, lo:hi] * cbuf_ref[l, start:start + rb, :]
        accs.append(acc)
    y = _layer_norm(jnp.concatenate(accs, axis=1), g_ref[...], b_ref[...])
    cy_ref[r:r + rb, :] = _silu(y).astype(BF16)


def _conv_finish(rows, rb, pc_ref, cy_ref, pw_w_ref, pw_b_ref, ycat_ref):
    z = jnp.dot(cy_ref[...], pw_w_ref[...], preferred_element_type=F32) + pw_b_ref[...]
    for r in range(0, rows, rb):
        ycat_ref[r:r + rb, 0:CONV_W] = (z[r:r + rb] * _silu(pc_ref[r:r + rb, 2 * CONV_W:3 * CONV_W])).astype(BF16)


def _lru_conv_gates(rows, rb, pr_ref, rbuf_ref, xc_ref, gb_ref, lcw_ref, lcb_ref, wg_ref, bg_ref):
    for l in range(LRU_W // LANES):
        lo, hi = l * LANES, (l + 1) * LANES
        rbuf_ref[l, LRU_HALO:LRU_HALO + rows, :] = pr_ref[:, lo:hi]
        for r in range(0, rows, rb):
            acc = jnp.broadcast_to(lcb_ref[:, lo:hi], (rb, LANES))
            for k in range(LRU_CONV_K):
                start = LRU_HALO - (LRU_CONV_K - 1) + k + r
                acc = acc + lcw_ref[k:k + 1, lo:hi] * rbuf_ref[l, start:start + rb, :]
            xc_ref[r:r + rb, lo:hi] = acc
    gb_ref[...] = jnp.dot(xc_ref[...].astype(BF16), wg_ref[...], preferred_element_type=F32) + bg_ref[...]


def _lru_group(r, carry, neg_c_softplus, pr_ref, xc_ref, gb_ref, ycat_ref, hout_ref=None):
    sub = lax.broadcasted_iota(jnp.int32, (SUBLANES, LRU_W), 0)
    x = xc_ref[r:r + SUBLANES, :]
    gate_r = jax.nn.sigmoid(gb_ref[r:r + SUBLANES, 0:LRU_W])
    gate_i = jax.nn.sigmoid(gb_ref[r:r + SUBLANES, LRU_W:2 * LRU_W])
    a = jnp.exp(gate_r * neg_c_softplus)
    u = jnp.sqrt(1.0 - a * a) * (gate_i * x)
    for d in (1, 2, 4):
        keep = sub >= d
        a_sh = jnp.where(keep, pltpu.roll(a, d, 0), 1.0)
        u_sh = jnp.where(keep, pltpu.roll(u, d, 0), 0.0)
        u = a * u_sh + u
        a = a * a_sh
    h = a * carry + u
    if hout_ref is not None:
        hout_ref[r:r + SUBLANES, :] = h
    ycat_ref[r:r + SUBLANES, CONV_W + ATT_W:OUT_IN] = (
        h * _silu(pr_ref[r:r + SUBLANES, LRU_W:2 * LRU_W])).astype(BF16)
    return jnp.broadcast_to(h[SUBLANES - 1:SUBLANES, :], (SUBLANES, LRU_W))


def _deepnorm_rows(resid, mixed, g_ref, b_ref):
    return _layer_norm(DEEPNORM_ALPHA * resid + mixed, g_ref[...], b_ref[...])


def _main_kernel(layer,
                 x_ref, cos_ref, s1_ref, s2_ref, lng_ref, lnb_ref, w_in_ref,
                 dw_w_ref, dw_b_ref, cg_ref, cb_ref, pw_w_ref, pw_b_ref, sink_ref,
                 lcw_ref, lcb_ref, wg_ref, bg_ref, lam_ref, w_out_ref, pg_ref, pb_ref,
                 cmeta_ref, kmeta_ref, vmeta_ref, rxmeta_ref, hlmeta_ref,
                 out_ref,
                 hb_ref, hres_ref, pc_ref, pa_ref, pr_ref, mix_ref, ycat_ref, cy_ref,
                 qbuf_ref, kbuf_ref, vbuf_ref, kmz_ref, vmz_ref, cbuf_ref, rbuf_ref, hcar_ref):
    first_layer = layer == 0
    T = T_CHUNK
    RB = ROW_BLOCK
    t = pl.program_id(1)

    @pl.when(t == 0)
    def _start_of_sequence():
        for l in range(CONV_W // LANES):
            cbuf_ref[l, 0:CONV_HALO - N_META, :] = jnp.zeros((CONV_HALO - N_META, LANES), F32)
            cbuf_ref[l, CONV_HALO - N_META:CONV_HALO, :] = cmeta_ref[:, l * LANES:(l + 1) * LANES]
            rbuf_ref[l, 0:LRU_HALO, :] = rxmeta_ref[N_META - LRU_HALO:N_META, l * LANES:(l + 1) * LANES]
        hcar_ref[...] = jnp.broadcast_to(hlmeta_ref[N_META - 1:N_META, :], (SUBLANES, LRU_W))
        kbuf_ref[:, 0:WINDOW, :] = jnp.zeros((N_KV, WINDOW, HEAD_DIM), BF16)
        vbuf_ref[:, 0:WINDOW, :] = jnp.zeros((N_KV, WINDOW, HEAD_DIM), BF16)
        pad = jnp.zeros((KPAD - KEYS - N_META, HEAD_DIM), BF16)
        for g in range(N_KV):
            kmz_ref[g, 0:N_META, :] = kmeta_ref[:, g * HEAD_DIM:(g + 1) * HEAD_DIM].astype(BF16)
            kmz_ref[g, N_META:KPAD - KEYS, :] = pad
            vmz_ref[g, 0:N_META, :] = vmeta_ref[:, g * HEAD_DIM:(g + 1) * HEAD_DIM].astype(BF16)
            vmz_ref[g, N_META:KPAD - KEYS, :] = pad

    def proj_chunk(dst_ref, c0, w0, n):
        dst_ref[:, c0:c0 + n] = jnp.dot(hb_ref[...], w_in_ref[:, w0:w0 + n], preferred_element_type=F32)

    resid_ref = hres_ref if first_layer else x_ref.at[0]
    for r in range(0, T, RB):
        h = x_ref[0, r:r + RB, :]
        if first_layer:
            h = _layer_norm(h, lng_ref[...], lnb_ref[...])
            hres_ref[r:r + RB, :] = h
        hb_ref[r:r + RB, :] = h.astype(BF16)
    proj_chunk(pc_ref, 0, 0, COL_ATT)

    for r in range(0, T, RB):
        _conv_rowblock(r, RB, pc_ref, cbuf_ref, dw_w_ref, dw_b_ref, cg_ref, cb_ref, cy_ref)
    for l in range(CONV_W // LANES):
        cbuf_ref[l, 0:CONV_HALO, :] = cbuf_ref[l, T:T + CONV_HALO, :]
    for c in range(0, COL_LRU - COL_ATT, PROJ_CHUNK):
        proj_chunk(pa_ref, c, COL_ATT + c, PROJ_CHUNK)
    for c in range(0, IN_TOTAL - COL_LRU, PROJ_CHUNK):
        proj_chunk(pr_ref, c, COL_LRU + c, PROJ_CHUNK)
    _conv_finish(T, RB, pc_ref, cy_ref, pw_w_ref, pw_b_ref, ycat_ref)
    mix_ref[...] = jnp.dot(ycat_ref[:, 0:CONV_W], w_out_ref[0:CONV_W, :], preferred_element_type=F32)

    cos, s1, s2 = cos_ref[...], s1_ref[...], s2_ref[...]
    for j in range(ATT_W // LANES):
        rot = _rotary(pa_ref[:, j * LANES:(j + 1) * LANES], cos, s1, s2) * ATT_SCALE
        qbuf_ref[2 * j, :, :] = rot[:, 0:HEAD_DIM].astype(BF16)
        qbuf_ref[2 * j + 1, :, :] = rot[:, HEAD_DIM:LANES].astype(BF16)
    for j in range(KV_W // LANES):
        rot = _rotary(pa_ref[:, ATT_W + j * LANES:ATT_W + (j + 1) * LANES], cos, s1, s2)
        kbuf_ref[2 * j, WINDOW:WINDOW + T, :] = rot[:, 0:HEAD_DIM].astype(BF16)
        kbuf_ref[2 * j + 1, WINDOW:WINDOW + T, :] = rot[:, HEAD_DIM:LANES].astype(BF16)
        v = pa_ref[:, ATT_W + KV_W + j * LANES:ATT_W + KV_W + (j + 1) * LANES]
        vbuf_ref[2 * j, WINDOW:WINDOW + T, :] = v[:, 0:HEAD_DIM].astype(BF16)
        vbuf_ref[2 * j + 1, WINDOW:WINDOW + T, :] = v[:, HEAD_DIM:LANES].astype(BF16)

    gb_ref = pc_ref.at[:, 0:2 * LRU_W]
    xc_ref = pc_ref.at[:, 2 * LRU_W:2 * LRU_W + LRU_W]
    _lru_conv_gates(T, RB, pr_ref, rbuf_ref, xc_ref, gb_ref, lcw_ref, lcb_ref, wg_ref, bg_ref)
    for l in range(LRU_W // LANES):
        rbuf_ref[l, 0:LRU_HALO, :] = rbuf_ref[l, T:T + LRU_HALO, :]

    hp = HEADS_PER_SCORE_TILE
    nrow = hp * HALF
    qi = lax.broadcasted_iota(jnp.int32, (nrow, KPAD), 0) % HALF
    kc = lax.broadcasted_iota(jnp.int32, (nrow, KPAD), 1)
    visible = ((kc > qi) & (kc <= qi + WINDOW)) | ((kc >= KEYS) & (kc < KEYS + N_META))
    rowp = lax.broadcasted_iota(jnp.int32, (nrow, 1), 0) // HALF
    a_gate0 = ATT_W + 2 * KV_W

    def attn_sub_block(g, blk, s):
        r0 = blk * WINDOW + s * HALF
        kw = jnp.concatenate([kbuf_ref[g, r0:r0 + KEYS, :], kmz_ref[g]], axis=0)
        vw = jnp.concatenate([vbuf_ref[g, r0:r0 + KEYS, :], vmz_ref[g]], axis=0)
        for h0 in range(g * GROUP, (g + 1) * GROUP, hp):
            sink_col = jnp.zeros((nrow, 1), F32)
            for p in range(hp):
                sink_col = jnp.where(rowp == p, sink_ref[layer, h0 + p], sink_col)
            q = jnp.concatenate([qbuf_ref[h0 + p, r0:r0 + HALF, :] for p in range(hp)], axis=0)
            sc = lax.dot_general(q, kw, (((1,), (1,)), ((), ())), preferred_element_type=F32)
            if blk == 0:
                first_key = jnp.where(t == 0, WINDOW - s * HALF, 0)
                sc = jnp.where(visible & (kc >= first_key), sc, NEG_INF)
            else:
                sc = jnp.where(visible, sc, NEG_INF)
            m = jnp.maximum(jnp.max(sc, axis=-1, keepdims=True), sink_col)
            pexp = jnp.exp(sc - m)
            den = jnp.sum(pexp, axis=-1, keepdims=True) + jnp.exp(sink_col - m)
            o = jnp.dot(pexp.astype(BF16), vw, preferred_element_type=F32) / den
            for p in range(hp):
                hcol = (h0 + p) * HEAD_DIM
                gate = _silu(pa_ref[r0:r0 + HALF, a_gate0 + hcol:a_gate0 + hcol + HEAD_DIM])
                ycat_ref[r0:r0 + HALF, CONV_W + hcol:CONV_W + hcol + HEAD_DIM] = (
                    o[p * HALF:(p + 1) * HALF, :] * gate).astype(BF16)

    sub_blocks = [(g, blk, s) for g in range(N_KV) for blk in range(T // WINDOW) for s in range(2)]
    groups_per_sub_block = (T // SUBLANES) // len(sub_blocks)
    neg_c_softplus = -LRU_C * jax.nn.softplus(-lam_ref[...])
    carry = hcar_ref[...]
    r_lru = 0
    for g, blk, s in sub_blocks:
        for _ in range(groups_per_sub_block):
            if r_lru < T:
                carry = _lru_group(r_lru, carry, neg_c_softplus, pr_ref, xc_ref, gb_ref, ycat_ref)
                r_lru += SUBLANES
        if r_lru == T:
            hcar_ref[...] = carry
            mix_ref[...] += jnp.dot(ycat_ref[:, CONV_W + ATT_W:OUT_IN], w_out_ref[CONV_W + ATT_W:OUT_IN, :],
                                    preferred_element_type=F32)
            r_lru += SUBLANES
        attn_sub_block(g, blk, s)
    assert r_lru == T + SUBLANES
    kbuf_ref[:, 0:WINDOW, :] = kbuf_ref[:, T:T + WINDOW, :]
    vbuf_ref[:, 0:WINDOW, :] = vbuf_ref[:, T:T + WINDOW, :]

    mix_ref[...] += jnp.dot(ycat_ref[:, CONV_W:CONV_W + ATT_W], w_out_ref[CONV_W:CONV_W + ATT_W, :],
                            preferred_element_type=F32)
    for r in range(0, T, RB):
        out_ref[0, r:r + RB, :] = _deepnorm_rows(resid_ref[r:r + RB, :], mix_ref[r:r + RB, :], pg_ref, pb_ref)


def _meta_kernel(layer,
                 x_ref, cos_ref, s1_ref, s2_ref, lng_ref, lnb_ref, w_in_ref,
                 dw_w_ref, dw_b_ref, cg_ref, cb_ref, pw_w_ref, pw_b_ref, sink_ref,
                 lcw_ref, lcb_ref, wg_ref, bg_ref, lam_ref, w_out_ref, pg_ref, pb_ref,
                 out_ref, cmeta_ref, kmeta_ref, vmeta_ref, rxmeta_ref, hlmeta_ref,
                 hb_ref, hres_ref, pc_ref, pa_ref, pr_ref, gb_ref, mixed_ref, ycat_ref, cy_ref,
                 cbuf_ref, rbuf_ref, xc_ref):
    M = N_META
    if layer == 0:
        hres_ref[...] = _layer_norm(x_ref[...], lng_ref[...], lnb_ref[...])
    else:
        hres_ref[...] = x_ref[...]
    hb_ref[...] = hres_ref[...].astype(BF16)

    pc_ref[...] = jnp.dot(hb_ref[...], w_in_ref[:, 0:COL_ATT], preferred_element_type=F32)
    for l in range(CONV_W // LANES):
        cbuf_ref[l, 0:CONV_HALO, :] = jnp.zeros((CONV_HALO, LANES), F32)
    _conv_rowblock(0, M, pc_ref, cbuf_ref, dw_w_ref, dw_b_ref, cg_ref, cb_ref, cy_ref)
    _conv_finish(M, M, pc_ref, cy_ref, pw_w_ref, pw_b_ref, ycat_ref)
    for l in range(CONV_W // LANES):
        cmeta_ref[:, l * LANES:(l + 1) * LANES] = cbuf_ref[l, CONV_HALO:CONV_HALO + M, :]

    pa_ref[...] = jnp.dot(hb_ref[...], w_in_ref[:, COL_ATT:COL_LRU], preferred_element_type=F32)
    cos, s1, s2 = cos_ref[...], s1_ref[...], s2_ref[...]
    qs = [_rotary(pa_ref[:, j * LANES:(j + 1) * LANES], cos, s1, s2) * ATT_SCALE for j in range(ATT_W // LANES)]
    for j in range(KV_W // LANES):
        kmeta_ref[:, j * LANES:(j + 1) * LANES] = _rotary(
            pa_ref[:, ATT_W + j * LANES:ATT_W + (j + 1) * LANES], cos, s1, s2)
    vmeta_ref[...] = pa_ref[:, ATT_W + KV_W:ATT_W + 2 * KV_W]
    nrow = GROUP * M
    qi = lax.broadcasted_iota(jnp.int32, (nrow, M), 0) % M
    kc = lax.broadcasted_iota(jnp.int32, (nrow, M), 1)
    rowp = lax.broadcasted_iota(jnp.int32, (nrow, 1), 0) // M
    a_gate0 = ATT_W + 2 * KV_W
    for g in range(N_KV):
        sink_col = jnp.zeros((nrow, 1), F32)
        for p in range(GROUP):
            sink_col = jnp.where(rowp == p, sink_ref[layer, g * GROUP + p], sink_col)
        heads = []
        for p in range(GROUP):
            hd = g * GROUP + p
            slab = qs[hd // 2]
            heads.append(slab[:, (hd % 2) * HEAD_DIM:(hd % 2 + 1) * HEAD_DIM].astype(BF16))
        q = jnp.concatenate(heads, axis=0)
        k = kmeta_ref[:, g * HEAD_DIM:(g + 1) * HEAD_DIM].astype(BF16)
        v = vmeta_ref[:, g * HEAD_DIM:(g + 1) * HEAD_DIM].astype(BF16)
        sc = lax.dot_general(q, k, (((1,), (1,)), ((), ())), preferred_element_type=F32)
        sc = jnp.where(kc <= qi, sc, NEG_INF)
        m = jnp.maximum(jnp.max(sc, axis=-1, keepdims=True), sink_col)
        pexp = jnp.exp(sc - m)
        den = jnp.sum(pexp, axis=-1, keepdims=True) + jnp.exp(sink_col - m)
        o = jnp.dot(pexp.astype(BF16), v, preferred_element_type=F32) / den
        for p in range(GROUP):
            hcol = (g * GROUP + p) * HEAD_DIM
            gate = _silu(pa_ref[:, a_gate0 + hcol:a_gate0 + hcol + HEAD_DIM])
            ycat_ref[:, CONV_W + hcol:CONV_W + hcol + HEAD_DIM] = (o[p * M:(p + 1) * M, :] * gate).astype(BF16)

    pr_ref[...] = jnp.dot(hb_ref[...], w_in_ref[:, COL_LRU:IN_TOTAL], preferred_element_type=F32)
    for l in range(LRU_W // LANES):
        rbuf_ref[l, 0:LRU_HALO, :] = jnp.zeros((LRU_HALO, LANES), F32)
    rxmeta_ref[...] = pr_ref[:, 0:LRU_W]
    _lru_conv_gates(M, M, pr_ref, rbuf_ref, xc_ref, gb_ref, lcw_ref, lcb_ref, wg_ref, bg_ref)
    neg_c_softplus = -LRU_C * jax.nn.softplus(-lam_ref[...])
    carry = jnp.zeros((SUBLANES, LRU_W), F32)
    for r in range(0, M, SUBLANES):
        carry = _lru_group(r, carry, neg_c_softplus, pr_ref, xc_ref, gb_ref, ycat_ref, hout_ref=hlmeta_ref)

    mixed_ref[...] = jnp.dot(ycat_ref[...], w_out_ref[...], preferred_element_type=F32)
    out_ref[...] = _deepnorm_rows(hres_ref[...], mixed_ref[...], pg_ref, pb_ref)


def _rope_tables(pos):
    half = ROT_DIM // 2
    inv_freq = ROPE_THETA ** (-jnp.arange(half, dtype=F32) / half)
    ang = pos.astype(F32)[:, None] * inv_freq[None, :]
    cos, sin = jnp.cos(ang), jnp.sin(ang)
    n = pos.shape[0]
    one = jnp.ones((n, HEAD_DIM - ROT_DIM), F32)
    zero = jnp.zeros((n, HEAD_DIM - ROT_DIM), F32)
    zh = jnp.zeros((n, half), F32)
    c64 = jnp.concatenate([cos, cos, one], axis=1)
    s1_64 = jnp.concatenate([-sin, zh, zero], axis=1)
    s2_64 = jnp.concatenate([zh, sin, zero], axis=1)
    rep = LANES // HEAD_DIM
    return jnp.tile(c64, (1, rep)), jnp.tile(s1_64, (1, rep)), jnp.tile(s2_64, (1, rep))


def _block_diag(w):
    h, d, _ = w.shape
    eye = jnp.eye(h, dtype=w.dtype)
    return (eye[:, None, :, None] * w[:, :, None, :]).reshape(h * d, h * d)


def _resident(shape, layer=None):
    nd = len(shape)
    if layer is None:
        return pl.BlockSpec(shape, lambda *_: (0,) * nd, pipeline_mode=pl.Buffered(1))
    return pl.BlockSpec((None,) + shape, lambda *_: (layer,) + (0,) * nd, pipeline_mode=pl.Buffered(1))


def _stacked_params(ln_in_g, ln_in_b, w_in, conv_dw_w, conv_dw_b, conv_ln_g, conv_ln_b, conv_pw_w, conv_pw_b,
                    attn_sinks, lru_conv_w, lru_conv_b, lru_wa, lru_ba, lru_wx, lru_bx, lru_lambda, w_out,
                    ln_post_g, ln_post_b):
    rows = lambda a: a.reshape(a.shape[0], 1, -1)
    w_gate = jnp.concatenate([jax.vmap(_block_diag)(lru_wa), jax.vmap(_block_diag)(lru_wx)], axis=-1).astype(BF16)
    b_gate = jnp.concatenate([lru_ba, lru_bx], axis=-1)
    return [ln_in_g.reshape(1, -1), ln_in_b.reshape(1, -1), w_in.astype(BF16),
            conv_dw_w, rows(conv_dw_b), rows(conv_ln_g), rows(conv_ln_b), conv_pw_w.astype(BF16), rows(conv_pw_b),
            attn_sinks,
            lru_conv_w, rows(lru_conv_b), w_gate, rows(b_gate), rows(lru_lambda),
            w_out.astype(BF16), rows(ln_post_g), rows(ln_post_b)]


def _param_specs(l):
    return [_resident((1, D_MODEL)), _resident((1, D_MODEL)), _resident((D_MODEL, IN_TOTAL), l),
            _resident((CONV_K, CONV_W), l), _resident((1, CONV_W), l), _resident((1, CONV_W), l),
            _resident((1, CONV_W), l), _resident((CONV_W, CONV_W), l), _resident((1, CONV_W), l),
            pl.BlockSpec(memory_space=pltpu.SMEM),
            _resident((LRU_CONV_K, LRU_W), l), _resident((1, LRU_W), l), _resident((LRU_W, 2 * LRU_W), l),
            _resident((1, 2 * LRU_W), l), _resident((1, LRU_W), l),
            _resident((OUT_IN, D_MODEL), l), _resident((1, D_MODEL), l), _resident((1, D_MODEL), l)]


def _meta_call(layer, h_meta, tables, params):
    M = N_META
    f = lambda *shape: jax.ShapeDtypeStruct(shape, F32)
    out_shape = [f(M, D_MODEL), f(M, CONV_W), f(M, KV_W), f(M, KV_W), f(M, LRU_W), f(M, LRU_W)]
    in_specs = [_resident((M, D_MODEL))] + [_resident((M, LANES))] * 3 + _param_specs(layer)
    out_specs = [pl.BlockSpec(s.shape, lambda i: (0, 0)) for s in out_shape]
    scratch = [pltpu.VMEM((M, D_MODEL), BF16), pltpu.VMEM((M, D_MODEL), F32),
               pltpu.VMEM((M, COL_ATT), F32), pltpu.VMEM((M, COL_LRU - COL_ATT), F32),
               pltpu.VMEM((M, 2 * LRU_W), F32), pltpu.VMEM((M, 2 * LRU_W), F32),
               pltpu.VMEM((M, D_MODEL), F32), pltpu.VMEM((M, OUT_IN), BF16), pltpu.VMEM((M, CONV_W), BF16),
               pltpu.VMEM((CONV_W // LANES, CONV_HALO + M, LANES), F32),
               pltpu.VMEM((LRU_W // LANES, LRU_HALO + M, LANES), F32),
               pltpu.VMEM((M, LRU_W), F32)]
    return pl.pallas_call(
        functools.partial(_meta_kernel, layer),
        grid=(1,), in_specs=in_specs, out_specs=out_specs, out_shape=out_shape, scratch_shapes=scratch,
        compiler_params=pltpu.CompilerParams(dimension_semantics=("arbitrary",), vmem_limit_bytes=VMEM_LIMIT_META),
        name=f"meta_layer{layer + 1}",
    )(h_meta, *tables, *params)


def _main_call(layer, h, tables, params, meta_state):
    B, S, _ = h.shape
    T = T_CHUNK
    M = N_META
    tok = pl.BlockSpec((1, T, D_MODEL), lambda b, t: (b, t, 0))
    tab = pl.BlockSpec((T, LANES), lambda b, t: (t, 0))
    in_specs = ([tok, tab, tab, tab] + _param_specs(layer)
                + [_resident((M, CONV_W)), _resident((M, KV_W)), _resident((M, KV_W)),
                   _resident((M, LRU_W)), _resident((M, LRU_W))])
    scratch = [pltpu.VMEM((T, D_MODEL), BF16), pltpu.VMEM((T, D_MODEL) if layer == 0 else (SUBLANES, LANES), F32),
               pltpu.VMEM((T, COL_ATT), F32), pltpu.VMEM((T, COL_LRU - COL_ATT), F32),
               pltpu.VMEM((T, 2 * LRU_W), F32),
               pltpu.VMEM((T, D_MODEL), F32), pltpu.VMEM((T, OUT_IN), BF16), pltpu.VMEM((T, CONV_W), BF16),
               pltpu.VMEM((N_HEADS, T, HEAD_DIM), BF16),
               pltpu.VMEM((N_KV, WINDOW + T, HEAD_DIM), BF16), pltpu.VMEM((N_KV, WINDOW + T, HEAD_DIM), BF16),
               pltpu.VMEM((N_KV, KPAD - KEYS, HEAD_DIM), BF16), pltpu.VMEM((N_KV, KPAD - KEYS, HEAD_DIM), BF16),
               pltpu.VMEM((CONV_W // LANES, CONV_HALO + T + SLAB_SKEW, LANES), F32),
               pltpu.VMEM((LRU_W // LANES, LRU_HALO + T + SLAB_SKEW, LANES), F32),
               pltpu.VMEM((SUBLANES, LRU_W), F32)]
    return pl.pallas_call(
        functools.partial(_main_kernel, layer),
        grid=(B, S // T), in_specs=in_specs, out_specs=tok,
        out_shape=jax.ShapeDtypeStruct((B, S, D_MODEL), F32), scratch_shapes=scratch,
        compiler_params=pltpu.CompilerParams(dimension_semantics=("arbitrary", "arbitrary"),
                                             vmem_limit_bytes=VMEM_LIMIT_MAIN),
        name=f"tokens_layer{layer + 1}",
    )(h, *tables, *params, *meta_state)


def kernel(x, meta_tokens, ln_in_g, ln_in_b, w_in, conv_dw_w, conv_dw_b, conv_ln_g, conv_ln_b, conv_pw_w, conv_pw_b,
           attn_sinks, lru_conv_w, lru_conv_b, lru_wa, lru_ba, lru_wx, lru_bx, lru_lambda, w_out, ln_post_g,
           ln_post_b):
    B, S, D = x.shape
    assert D == D_MODEL and S % T_CHUNK == 0 and w_in.shape == (DEPTH, D_MODEL, IN_TOTAL)
    meta_tables = _rope_tables(jnp.arange(N_META, dtype=jnp.int32))
    tok_tables = _rope_tables(N_META + jnp.arange(S, dtype=jnp.int32))
    params = _stacked_params(ln_in_g, ln_in_b, w_in, conv_dw_w, conv_dw_b, conv_ln_g, conv_ln_b, conv_pw_w, conv_pw_b,
                             attn_sinks, lru_conv_w, lru_conv_b, lru_wa, lru_ba, lru_wx, lru_bx, lru_lambda, w_out,
                             ln_post_g, ln_post_b)
    h, h_meta = x, meta_tokens.astype(x.dtype)
    for l in range(DEPTH):
        h_meta, *meta_state = _meta_call(l, h_meta, meta_tables, params)
        h = _main_call(l, h, tok_tables, params, meta_state)
    return h
```

```python
import functools

import jax
import jax.numpy as jnp
from jax import lax
from jax.experimental import pallas as pl
from jax.experimental.pallas import tpu as pltpu

F32 = jnp.float32
BF16 = jnp.bfloat16

D_MODEL = 2048
N_META = 16
CONV_W = 512
CONV_K = 31
HEAD_DIM = 64
N_HEADS = 16
N_KV = 4
GROUP = N_HEADS // N_KV
ATT_W = N_HEADS * HEAD_DIM
KV_W = N_KV * HEAD_DIM
WINDOW = 128
ROT_DIM = HEAD_DIM // 4
ROPE_THETA = 500000.0
LRU_W = 512
LRU_HEADS = 8
LRU_CONV_K = 4
LRU_C = 8.0
IN_TOTAL = 3 * CONV_W + 2 * ATT_W + 2 * KV_W + 2 * LRU_W
COL_ATT = 3 * CONV_W
COL_LRU = COL_ATT + 2 * ATT_W + 2 * KV_W
OUT_IN = CONV_W + ATT_W + LRU_W
LN_EPS = 1e-5
DEPTH = 2
DEEPNORM_ALPHA = (2.0 * DEPTH) ** 0.25
NEG_INF = -1e30
ATT_SCALE = HEAD_DIM ** -0.5

LANES = 128
SUBLANES = 8
T_CHUNK = 256
CONV_HALO = 32
LRU_HALO = SUBLANES
HALF = WINDOW // 2
KEYS = WINDOW + HALF
KPAD = 256
SLAB_SKEW = 8
ROW_BLOCK = 32
PROJ_CHUNK = 512
HEADS_PER_SCORE_TILE = 2
VMEM_LIMIT_MAIN = 60 * 1024 * 1024
VMEM_LIMIT_META = 48 * 1024 * 1024


def _silu(x):
    return x * jax.nn.sigmoid(x)


def _layer_norm(x, g, b):
    mu = jnp.mean(x, axis=-1, keepdims=True)
    xc = x - mu
    var = jnp.mean(xc * xc, axis=-1, keepdims=True)
    return xc * lax.rsqrt(var + LN_EPS) * g + b


def _rotary(x, cos, s1, s2):
    return x * cos + pltpu.roll(x, LANES - ROT_DIM // 2, 1) * s1 + pltpu.roll(x, ROT_DIM // 2, 1) * s2


def _conv_rowblock(r, rb, pc_ref, cbuf_ref, dw_w_ref, dw_b_ref, g_ref, b_ref, cy_ref):
    accs = []
    for l in range(CONV_W // LANES):
        lo, hi = l * LANES, (l + 1) * LANES
        cbuf_ref[l, CONV_HALO + r:CONV_HALO + r + rb, :] = (
            pc_ref[r:r + rb, lo:hi] * jax.nn.sigmoid(pc_ref[r:r + rb, CONV_W + lo:CONV_W + hi]))
        acc = jnp.broadcast_to(dw_b_ref[:, lo:hi], (rb, LANES))
        for k in range(CONV_K):
            start = CONV_HALO - (CONV_K - 1) + k + r
            acc = acc + dw_w_ref[k:k + 1, lo:hi] * cbuf_ref[l, start:start + rb, :]
        accs.append(acc)
    y = _layer_norm(jnp.concatenate(accs, axis=1), g_ref[...], b_ref[...])
    cy_ref[r:r + rb, :] = _silu(y).astype(BF16)


def _conv_finish(rows, rb, pc_ref, cy_ref, pw_w_ref, pw_b_ref, ycat_ref):
    z = jnp.dot(cy_ref[...], pw_w_ref[...], preferred_element_type=F32) + pw_b_ref[...]
    for r in range(0, rows, rb):
        ycat_ref[r:r + rb, 0:CONV_W] = (z[r:r + rb] * _silu(pc_ref[r:r + rb, 2 * CONV_W:3 * CONV_W])).astype(BF16)


def _lru_conv_gates(rows, rb, pr_ref, rbuf_ref, xc_ref, gb_ref, lcw_ref, lcb_ref, wg_ref, bg_ref):
    for l in range(LRU_W // LANES):
        lo, hi = l * LANES, (l + 1) * LANES
        rbuf_ref[l, LRU_HALO:LRU_HALO + rows, :] = pr_ref[:, lo:hi]
        for r in range(0, rows, rb):
            acc = jnp.broadcast_to(lcb_ref[:, lo:hi], (rb, LANES))
            for k in range(LRU_CONV_K):
                start = LRU_HALO - (LRU_CONV_K - 1) + k + r
                acc = acc + lcw_ref[k:k + 1, lo:hi] * rbuf_ref[l, start:start + rb, :]
            xc_ref[r:r + rb, lo:hi] = acc
    gb_ref[...] = jnp.dot(xc_ref[...].astype(BF16), wg_ref[...], preferred_element_type=F32) + bg_ref[...]


def _lru_group(r, carry, neg_c_softplus, pr_ref, xc_ref, gb_ref, ycat_ref, hout_ref=None):
    sub = lax.broadcasted_iota(jnp.int32, (SUBLANES, LRU_W), 0)
    x = xc_ref[r:r + SUBLANES, :]
    gate_r = jax.nn.sigmoid(gb_ref[r:r + SUBLANES, 0:LRU_W])
    gate_i = jax.nn.sigmoid(gb_ref[r:r + SUBLANES, LRU_W:2 * LRU_W])
    a = jnp.exp(gate_r * neg_c_softplus)
    u = jnp.sqrt(1.0 - a * a) * (gate_i * x)
    for d in (1, 2, 4):
        keep = sub >= d
        a_sh = jnp.where(keep, pltpu.roll(a, d, 0), 1.0)
        u_sh = jnp.where(keep, pltpu.roll(u, d, 0), 0.0)
        u = a * u_sh + u
        a = a * a_sh
    h = a * carry + u
    if hout_ref is not None:
        hout_ref[r:r + SUBLANES, :] = h
    ycat_ref[r:r + SUBLANES, CONV_W + ATT_W:OUT_IN] = (
        h * _silu(pr_ref[r:r + SUBLANES, LRU_W:2 * LRU_W])).astype(BF16)
    return jnp.broadcast_to(h[SUBLANES - 1:SUBLANES, :], (SUBLANES, LRU_W))


def _deepnorm_rows(resid, mixed, g_ref, b_ref):
    return _layer_norm(DEEPNORM_ALPHA * resid + mixed, g_ref[...], b_ref[...])


def _main_kernel(layer,
                 x_ref, cos_ref, s1_ref, s2_ref, lng_ref, lnb_ref, w_in_ref,
                 dw_w_ref, dw_b_ref, cg_ref, cb_ref, pw_w_ref, pw_b_ref, sink_ref,
                 lcw_ref, lcb_ref, wg_ref, bg_ref, lam_ref, w_out_ref, pg_ref, pb_ref,
                 cmeta_ref, kmeta_ref, vmeta_ref, rxmeta_ref, hlmeta_ref,
                 out_ref,
                 hb_ref, hres_ref, pc_ref, pa_ref, pr_ref, mix_ref, ycat_ref, cy_ref,
                 qbuf_ref, kbuf_ref, vbuf_ref, kmz_ref, vmz_ref, cbuf_ref, rbuf_ref, hcar_ref):
    first_layer = layer == 0
    T = T_CHUNK
    RB = ROW_BLOCK
    t = pl.program_id(1)

    @pl.when(t == 0)
    def _start_of_sequence():
        for l in range(CONV_W // LANES):
            cbuf_ref[l, 0:CONV_HALO - N_META, :] = jnp.zeros((CONV_HALO - N_META, LANES), F32)
            cbuf_ref[l, CONV_HALO - N_META:CONV_HALO, :] = cmeta_ref[:, l * LANES:(l + 1) * LANES]
            rbuf_ref[l, 0:LRU_HALO, :] = rxmeta_ref[N_META - LRU_HALO:N_META, l * LANES:(l + 1) * LANES]
        hcar_ref[...] = jnp.broadcast_to(hlmeta_ref[N_META - 1:N_META, :], (SUBLANES, LRU_W))
        kbuf_ref[:, 0:WINDOW, :] = jnp.zeros((N_KV, WINDOW, HEAD_DIM), BF16)
        vbuf_ref[:, 0:WINDOW, :] = jnp.zeros((N_KV, WINDOW, HEAD_DIM), BF16)
        pad = jnp.zeros((KPAD - KEYS - N_META, HEAD_DIM), BF16)
        for g in range(N_KV):
            kmz_ref[g, 0:N_META, :] = kmeta_ref[:, g * HEAD_DIM:(g + 1) * HEAD_DIM].astype(BF16)
            kmz_ref[g, N_META:KPAD - KEYS, :] = pad
            vmz_ref[g, 0:N_META, :] = vmeta_ref[:, g * HEAD_DIM:(g + 1) * HEAD_DIM].astype(BF16)
            vmz_ref[g, N_META:KPAD - KEYS, :] = pad

    def proj_chunk(dst_ref, c0, w0, n):
        dst_ref[:, c0:c0 + n] = jnp.dot(hb_ref[...], w_in_ref[:, w0:w0 + n], preferred_element_type=F32)

    resid_ref = hres_ref if first_layer else x_ref.at[0]
    for r in range(0, T, RB):
        h = x_ref[0, r:r + RB, :]
        if first_layer:
            h = _layer_norm(h, lng_ref[...], lnb_ref[...])
            hres_ref[r:r + RB, :] = h
        hb_ref[r:r + RB, :] = h.astype(BF16)
    proj_chunk(pc_ref, 0, 0, COL_ATT)

    for r in range(0, T, RB):
        _conv_rowblock(r, RB, pc_ref, cbuf_ref, dw_w_ref, dw_b_ref, cg_ref, cb_ref, cy_ref)
    for l in range(CONV_W // LANES):
        cbuf_ref[l, 0:CONV_HALO, :] = cbuf_ref[l, T:T + CONV_HALO, :]
    for c in range(0, COL_LRU - COL_ATT, PROJ_CHUNK):
        proj_chunk(pa_ref, c, COL_ATT + c, PROJ_CHUNK)
    for c in range(0, IN_TOTAL - COL_LRU, PROJ_CHUNK):
        proj_chunk(pr_ref, c, COL_LRU + c, PROJ_CHUNK)
    _conv_finish(T, RB, pc_ref, cy_ref, pw_w_ref, pw_b_ref, ycat_ref)
    mix_ref[...] = jnp.dot(ycat_ref[:, 0:CONV_W], w_out_ref[0:CONV_W, :], preferred_element_type=F32)

    cos, s1, s2 = cos_ref[...], s1_ref[...], s2_ref[...]
    for j in range(ATT_W // LANES):
        rot = _rotary(pa_ref[:, j * LANES:(j + 1) * LANES], cos, s1, s2) * ATT_SCALE
        qbuf_ref[2 * j, 0:T, :] = rot[:, 0:HEAD_DIM].astype(BF16)
        qbuf_ref[2 * j + 1, 0:T, :] = rot[:, HEAD_DIM:LANES].astype(BF16)
    for j in range(KV_W // LANES):
        rot = _rotary(pa_ref[:, ATT_W + j * LANES:ATT_W + (j + 1) * LANES], cos, s1, s2)
        kbuf_ref[2 * j, WINDOW:WINDOW + T, :] = rot[:, 0:HEAD_DIM].astype(BF16)
        kbuf_ref[2 * j + 1, WINDOW:WINDOW + T, :] = rot[:, HEAD_DIM:LANES].astype(BF16)
        v = pa_ref[:, ATT_W + KV_W + j * LANES:ATT_W + KV_W + (j + 1) * LANES]
        vbuf_ref[2 * j, WINDOW:WINDOW + T, :] = v[:, 0:HEAD_DIM].astype(BF16)
        vbuf_ref[2 * j + 1, WINDOW:WINDOW + T, :] = v[:, HEAD_DIM:LANES].astype(BF16)

    gb_ref = pc_ref.at[:, 0:2 * LRU_W]
    xc_ref = pc_ref.at[:, 2 * LRU_W:2 * LRU_W + LRU_W]
    _lru_conv_gates(T, RB, pr_ref, rbuf_ref, xc_ref, gb_ref, lcw_ref, lcb_ref, wg_ref, bg_ref)
    for l in range(LRU_W // LANES):
        rbuf_ref[l, 0:LRU_HALO, :] = rbuf_ref[l, T:T + LRU_HALO, :]

    hp = HEADS_PER_SCORE_TILE
    nrow = hp * HALF
    qi = lax.broadcasted_iota(jnp.int32, (nrow, KPAD), 0) % HALF
    kc = lax.broadcasted_iota(jnp.int32, (nrow, KPAD), 1)
    visible = ((kc > qi) & (kc <= qi + WINDOW)) | ((kc >= KEYS) & (kc < KEYS + N_META))
    rowp = lax.broadcasted_iota(jnp.int32, (nrow, 1), 0) // HALF
    a_gate0 = ATT_W + 2 * KV_W

    def attn_sub_block(g, blk, s):
        r0 = blk * WINDOW + s * HALF
        kw = jnp.concatenate([kbuf_ref[g, r0:r0 + KEYS, :], kmz_ref[g]], axis=0)
        vw = jnp.concatenate([vbuf_ref[g, r0:r0 + KEYS, :], vmz_ref[g]], axis=0)
        for h0 in range(g * GROUP, (g + 1) * GROUP, hp):
            sink_col = jnp.zeros((nrow, 1), F32)
            for p in range(hp):
                sink_col = jnp.where(rowp == p, sink_ref[layer, h0 + p], sink_col)
            q = jnp.concatenate([qbuf_ref[h0 + p, r0:r0 + HALF, :] for p in range(hp)], axis=0)
            sc = lax.dot_general(q, kw, (((1,), (1,)), ((), ())), preferred_element_type=F32)
            if blk == 0:
                first_key = jnp.where(t == 0, WINDOW - s * HALF, 0)
                sc = jnp.where(visible & (kc >= first_key), sc, NEG_INF)
            else:
                sc = jnp.where(visible, sc, NEG_INF)
            m = jnp.maximum(jnp.max(sc, axis=-1, keepdims=True), sink_col)
            pexp = jnp.exp(sc - m)
            den = jnp.sum(pexp, axis=-1, keepdims=True) + jnp.exp(sink_col - m)
            o = jnp.dot(pexp.astype(BF16), vw, preferred_element_type=F32) / den
            for p in range(hp):
                hcol = (h0 + p) * HEAD_DIM
                gate = _silu(pa_ref[r0:r0 + HALF, a_gate0 + hcol:a_gate0 + hcol + HEAD_DIM])
                ycat_ref[r0:r0 + HALF, CONV_W + hcol:CONV_W + hcol + HEAD_DIM] = (
                    o[p * HALF:(p + 1) * HALF, :] * gate).astype(BF16)

    sub_blocks = [(g, blk, s) for g in range(N_KV) for blk in range(T // WINDOW) for s in range(2)]
    groups_per_sub_block = (T // SUBLANES) // len(sub_blocks)
    neg_c_softplus = -LRU_C * jax.nn.softplus(-lam_ref[...])
    carry = hcar_ref[...]
    r_lru = 0
    for g, blk, s in sub_blocks:
        for _ in range(groups_per_sub_block):
            if r_lru < T:
                carry = _lru_group(r_lru, carry, neg_c_softplus, pr_ref, xc_ref, gb_ref, ycat_ref)
                r_lru += SUBLANES
        if r_lru == T:
            hcar_ref[...] = carry
            mix_ref[...] += jnp.dot(ycat_ref[:, CONV_W + ATT_W:OUT_IN], w_out_ref[CONV_W + ATT_W:OUT_IN, :],
                                    preferred_element_type=F32)
            r_lru += SUBLANES
        attn_sub_block(g, blk, s)
    assert r_lru == T + SUBLANES
    kbuf_ref[:, 0:WINDOW, :] = kbuf_ref[:, T:T + WINDOW, :]
    vbuf_ref[:, 0:WINDOW, :] = vbuf_ref[:, T:T + WINDOW, :]

    mix_ref[...] += jnp.dot(ycat_ref[:, CONV_W:CONV_W + ATT_W], w_out_ref[CONV_W:CONV_W + ATT_W, :],
                            preferred_element_type=F32)
    for r in range(0, T, RB):
        out_ref[0, r:r + RB, :] = _deepnorm_rows(resid_ref[r:r + RB, :], mix_ref[r:r + RB, :], pg_ref, pb_ref)


def _meta_kernel(layer,
                 x_ref, cos_ref, s1_ref, s2_ref, lng_ref, lnb_ref, w_in_ref,
                 dw_w_ref, dw_b_ref, cg_ref, cb_ref, pw_w_ref, pw_b_ref, sink_ref,
                 lcw_ref, lcb_ref, wg_ref, bg_ref, lam_ref, w_out_ref, pg_ref, pb_ref,
                 out_ref, cmeta_ref, kmeta_ref, vmeta_ref, rxmeta_ref, hlmeta_ref,
                 hb_ref, hres_ref, pc_ref, pa_ref, pr_ref, gb_ref, mixed_ref, ycat_ref, cy_ref,
                 cbuf_ref, rbuf_ref, xc_ref):
    M = N_META
    if layer == 0:
        hres_ref[...] = _layer_norm(x_ref[...], lng_ref[...], lnb_ref[...])
    else:
        hres_ref[...] = x_ref[...]
    hb_ref[...] = hres_ref[...].astype(BF16)

    pc_ref[...] = jnp.dot(hb_ref[...], w_in_ref[:, 0:COL_ATT], preferred_element_type=F32)
    for l in range(CONV_W // LANES):
        cbuf_ref[l, 0:CONV_HALO, :] = jnp.zeros((CONV_HALO, LANES), F32)
    _conv_rowblock(0, M, pc_ref, cbuf_ref, dw_w_ref, dw_b_ref, cg_ref, cb_ref, cy_ref)
    _conv_finish(M, M, pc_ref, cy_ref, pw_w_ref, pw_b_ref, ycat_ref)
    for l in range(CONV_W // LANES):
        cmeta_ref[:, l * LANES:(l + 1) * LANES] = cbuf_ref[l, CONV_HALO:CONV_HALO + M, :]

    pa_ref[...] = jnp.dot(hb_ref[...], w_in_ref[:, COL_ATT:COL_LRU], preferred_element_type=F32)
    cos, s1, s2 = cos_ref[...], s1_ref[...], s2_ref[...]
    qs = [_rotary(pa_ref[:, j * LANES:(j + 1) * LANES], cos, s1, s2) * ATT_SCALE for j in range(ATT_W // LANES)]
    for j in range(KV_W // LANES):
        kmeta_ref[:, j * LANES:(j + 1) * LANES] = _rotary(
            pa_ref[:, ATT_W + j * LANES:ATT_W + (j + 1) * LANES], cos, s1, s2)
    vmeta_ref[...] = pa_ref[:, ATT_W + KV_W:ATT_W + 2 * KV_W]
    nrow = GROUP * M
    qi = lax.broadcasted_iota(jnp.int32, (nrow, M), 0) % M
    kc = lax.broadcasted_iota(jnp.int32, (nrow, M), 1)
    rowp = lax.broadcasted_iota(jnp.int32, (nrow, 1), 0) // M
    a_gate0 = ATT_W + 2 * KV_W
    for g in range(N_KV):
        sink_col = jnp.zeros((nrow, 1), F32)
        for p in range(GROUP):
            sink_col = jnp.where(rowp == p, sink_ref[layer, g * GROUP + p], sink_col)
        heads = []
        for p in range(GROUP):
            hd = g * GROUP + p
            slab = qs[hd // 2]
            heads.append(slab[:, (hd % 2) * HEAD_DIM:(hd % 2 + 1) * HEAD_DIM].astype(BF16))
        q = jnp.concatenate(heads, axis=0)
        k = kmeta_ref[:, g * HEAD_DIM:(g + 1) * HEAD_DIM].astype(BF16)
        v = vmeta_ref[:, g * HEAD_DIM:(g + 1) * HEAD_DIM].astype(BF16)
        sc = lax.dot_general(q, k, (((1,), (1,)), ((), ())), preferred_element_type=F32)
        sc = jnp.where(kc <= qi, sc, NEG_INF)
        m = jnp.maximum(jnp.max(sc, axis=-1, keepdims=True), sink_col)
        pexp = jnp.exp(sc - m)
        den = jnp.sum(pexp, axis=-1, keepdims=True) + jnp.exp(sink_col - m)
        o = jnp.dot(pexp.astype(BF16), v, preferred_element_type=F32) / den
        for p in range(GROUP):
            hcol = (g * GROUP + p) * HEAD_DIM
            gate = _silu(pa_ref[:, a_gate0 + hcol:a_gate0 + hcol + HEAD_DIM])
            ycat_ref[:, CONV_W + hcol:CONV_W + hcol + HEAD_DIM] = (o[p * M:(p + 1) * M, :] * gate).astype(BF16)

    pr_ref[...] = jnp.dot(hb_ref[...], w_in_ref[:, COL_LRU:IN_TOTAL], preferred_element_type=F32)
    for l in range(LRU_W // LANES):
        rbuf_ref[l, 0:LRU_HALO, :] = jnp.zeros((LRU_HALO, LANES), F32)
    rxmeta_ref[...] = pr_ref[:, 0:LRU_W]
    _lru_conv_gates(M, M, pr_ref, rbuf_ref, xc_ref, gb_ref, lcw_ref, lcb_ref, wg_ref, bg_ref)
    neg_c_softplus = -LRU_C * jax.nn.softplus(-lam_ref[...])
    carry = jnp.zeros((SUBLANES, LRU_W), F32)
    for r in range(0, M, SUBLANES):
        carry = _lru_group(r, carry, neg_c_softplus, pr_ref, xc_ref, gb_ref, ycat_ref, hout_ref=hlmeta_ref)

    mixed_ref[...] = jnp.dot(ycat_ref[...], w_out_ref[...], preferred_element_type=F32)
    out_ref[...] = _deepnorm_rows(hres_ref[...], mixed_ref[...], pg_ref, pb_ref)


def _rope_tables(pos):
    half = ROT_DIM // 2
    inv_freq = ROPE_THETA ** (-jnp.arange(half, dtype=F32) / half)
    ang = pos.astype(F32)[:, None] * inv_freq[None, :]
    cos, sin = jnp.cos(ang), jnp.sin(ang)
    n = pos.shape[0]
    one = jnp.ones((n, HEAD_DIM - ROT_DIM), F32)
    zero = jnp.zeros((n, HEAD_DIM - ROT_DIM), F32)
    zh = jnp.zeros((n, half), F32)
    c64 = jnp.concatenate([cos, cos, one], axis=1)
    s1_64 = jnp.concatenate([-sin, zh, zero], axis=1)
    s2_64 = jnp.concatenate([zh, sin, zero], axis=1)
    rep = LANES // HEAD_DIM
    return jnp.tile(c64, (1, rep)), jnp.tile(s1_64, (1, rep)), jnp.tile(s2_64, (1, rep))


def _block_diag(w):
    h, d, _ = w.shape
    eye = jnp.eye(h, dtype=w.dtype)
    return (eye[:, None, :, None] * w[:, :, None, :]).reshape(h * d, h * d)


def _resident(shape, layer=None):
    nd = len(shape)
    if layer is None:
        return pl.BlockSpec(shape, lambda *_: (0,) * nd, pipeline_mode=pl.Buffered(1))
    return pl.BlockSpec((None,) + shape, lambda *_: (layer,) + (0,) * nd, pipeline_mode=pl.Buffered(1))


def _stacked_params(ln_in_g, ln_in_b, w_in, conv_dw_w, conv_dw_b, conv_ln_g, conv_ln_b, conv_pw_w, conv_pw_b,
                    attn_sinks, lru_conv_w, lru_conv_b, lru_wa, lru_ba, lru_wx, lru_bx, lru_lambda, w_out,
                    ln_post_g, ln_post_b):
    rows = lambda a: a.reshape(a.shape[0], 1, -1)
    w_gate = jnp.concatenate([jax.vmap(_block_diag)(lru_wa), jax.vmap(_block_diag)(lru_wx)], axis=-1).astype(BF16)
    b_gate = jnp.concatenate([lru_ba, lru_bx], axis=-1)
    return [ln_in_g.reshape(1, -1), ln_in_b.reshape(1, -1), w_in.astype(BF16),
            conv_dw_w, rows(conv_dw_b), rows(conv_ln_g), rows(conv_ln_b), conv_pw_w.astype(BF16), rows(conv_pw_b),
            attn_sinks,
            lru_conv_w, rows(lru_conv_b), w_gate, rows(b_gate), rows(lru_lambda),
            w_out.astype(BF16), rows(ln_post_g), rows(ln_post_b)]


def _param_specs(l):
    return [_resident((1, D_MODEL)), _resident((1, D_MODEL)), _resident((D_MODEL, IN_TOTAL), l),
            _resident((CONV_K, CONV_W), l), _resident((1, CONV_W), l), _resident((1, CONV_W), l),
            _resident((1, CONV_W), l), _resident((CONV_W, CONV_W), l), _resident((1, CONV_W), l),
            pl.BlockSpec(memory_space=pltpu.SMEM),
            _resident((LRU_CONV_K, LRU_W), l), _resident((1, LRU_W), l), _resident((LRU_W, 2 * LRU_W), l),
            _resident((1, 2 * LRU_W), l), _resident((1, LRU_W), l),
            _resident((OUT_IN, D_MODEL), l), _resident((1, D_MODEL), l), _resident((1, D_MODEL), l)]


def _meta_call(layer, h_meta, tables, params):
    M = N_META
    f = lambda *shape: jax.ShapeDtypeStruct(shape, F32)
    out_shape = [f(M, D_MODEL), f(M, CONV_W), f(M, KV_W), f(M, KV_W), f(M, LRU_W), f(M, LRU_W)]
    in_specs = [_resident((M, D_MODEL))] + [_resident((M, LANES))] * 3 + _param_specs(layer)
    out_specs = [pl.BlockSpec(s.shape, lambda i: (0, 0)) for s in out_shape]
    scratch = [pltpu.VMEM((M, D_MODEL), BF16), pltpu.VMEM((M, D_MODEL), F32),
               pltpu.VMEM((M, COL_ATT), F32), pltpu.VMEM((M, COL_LRU - COL_ATT), F32),
               pltpu.VMEM((M, 2 * LRU_W), F32), pltpu.VMEM((M, 2 * LRU_W), F32),
               pltpu.VMEM((M, D_MODEL), F32), pltpu.VMEM((M, OUT_IN), BF16), pltpu.VMEM((M, CONV_W), BF16),
               pltpu.VMEM((CONV_W // LANES, CONV_HALO + M, LANES), F32),
               pltpu.VMEM((LRU_W // LANES, LRU_HALO + M, LANES), F32),
               pltpu.VMEM((M, LRU_W), F32)]
    return pl.pallas_call(
        functools.partial(_meta_kernel, layer),
        grid=(1,), in_specs=in_specs, out_specs=out_specs, out_shape=out_shape, scratch_shapes=scratch,
        compiler_params=pltpu.CompilerParams(dimension_semantics=("arbitrary",), vmem_limit_bytes=VMEM_LIMIT_META),
        name=f"meta_layer{layer + 1}",
    )(h_meta, *tables, *params)


def _main_call(layer, h, tables, params, meta_state):
    B, S, _ = h.shape
    T = T_CHUNK
    M = N_META
    tok = pl.BlockSpec((1, T, D_MODEL), lambda b, t: (b, t, 0))
    tab = pl.BlockSpec((T, LANES), lambda b, t: (t, 0))
    in_specs = ([tok, tab, tab, tab] + _param_specs(layer)
                + [_resident((M, CONV_W)), _resident((M, KV_W)), _resident((M, KV_W)),
                   _resident((M, LRU_W)), _resident((M, LRU_W))])
    scratch = [pltpu.VMEM((T, D_MODEL), BF16), pltpu.VMEM((T, D_MODEL) if layer == 0 else (SUBLANES, LANES), F32),
               pltpu.VMEM((T, COL_ATT), F32), pltpu.VMEM((T, COL_LRU - COL_ATT), F32),
               pltpu.VMEM((T, 2 * LRU_W), F32),
               pltpu.VMEM((T, D_MODEL), F32), pltpu.VMEM((T, OUT_IN), BF16), pltpu.VMEM((T, CONV_W), BF16),
               pltpu.VMEM((N_HEADS, T + 2 * SLAB_SKEW, HEAD_DIM), BF16),
               pltpu.VMEM((N_KV, WINDOW + T + 2 * SLAB_SKEW, HEAD_DIM), BF16),
               pltpu.VMEM((N_KV, WINDOW + T + 2 * SLAB_SKEW, HEAD_DIM), BF16),
               pltpu.VMEM((N_KV, KPAD - KEYS, HEAD_DIM), BF16), pltpu.VMEM((N_KV, KPAD - KEYS, HEAD_DIM), BF16),
               pltpu.VMEM((CONV_W // LANES, CONV_HALO + T + SLAB_SKEW, LANES), F32),
               pltpu.VMEM((LRU_W // LANES, LRU_HALO + T + SLAB_SKEW, LANES), F32),
               pltpu.VMEM((SUBLANES, LRU_W), F32)]
    return pl.pallas_call(
        functools.partial(_main_kernel, layer),
        grid=(B, S // T), in_specs=in_specs, out_specs=tok,
        out_shape=jax.ShapeDtypeStruct((B, S, D_MODEL), F32), scratch_shapes=scratch,
        compiler_params=pltpu.CompilerParams(dimension_semantics=("arbitrary", "arbitrary"),
                                             vmem_limit_bytes=VMEM_LIMIT_MAIN),
        name=f"tokens_layer{layer + 1}",
    )(h, *tables, *params, *meta_state)


def kernel(x, meta_tokens, ln_in_g, ln_in_b, w_in, conv_dw_w, conv_dw_b, conv_ln_g, conv_ln_b, conv_pw_w, conv_pw_b,
           attn_sinks, lru_conv_w, lru_conv_b, lru_wa, lru_ba, lru_wx, lru_bx, lru_lambda, w_out, ln_post_g,
           ln_post_b):
    B, S, D = x.shape
    assert D == D_MODEL and S % T_CHUNK == 0 and w_in.shape == (DEPTH, D_MODEL, IN_TOTAL)
    meta_tables = _rope_tables(jnp.arange(N_META, dtype=jnp.int32))
    tok_tables = _rope_tables(N_META + jnp.arange(S, dtype=jnp.int32))
    params = _stacked_params(ln_in_g, ln_in_b, w_in, conv_dw_w, conv_dw_b, conv_ln_g, conv_ln_b, conv_pw_w, conv_pw_b,
                             attn_sinks, lru_conv_w, lru_conv_b, lru_wa, lru_ba, lru_wx, lru_bx, lru_lambda, w_out,
                             ln_post_g, ln_post_b)
    h, h_meta = x, meta_tokens.astype(x.dtype)
    for l in range(DEPTH):
        h_meta, *meta_state = _meta_call(l, h_meta, meta_tables, params)
        h = _main_call(l, h, tok_tables, params, meta_state)
    return h
```

```python
import functools

import jax
import jax.numpy as jnp
from jax import lax
from jax.experimental import pallas as pl
from jax.experimental.pallas import tpu as pltpu

F32 = jnp.float32
BF16 = jnp.bfloat16

D_MODEL = 2048
N_META = 16
CONV_W = 512
CONV_K = 31
HEAD_DIM = 64
N_HEADS = 16
N_KV = 4
GROUP = N_HEADS // N_KV
ATT_W = N_HEADS * HEAD_DIM
KV_W = N_KV * HEAD_DIM
WINDOW = 128
ROT_DIM = HEAD_DIM // 4
ROPE_THETA = 500000.0
LRU_W = 512
LRU_HEADS = 8
LRU_CONV_K = 4
LRU_C = 8.0
IN_TOTAL = 3 * CONV_W + 2 * ATT_W + 2 * KV_W + 2 * LRU_W
COL_ATT = 3 * CONV_W
COL_LRU = COL_ATT + 2 * ATT_W + 2 * KV_W
OUT_IN = CONV_W + ATT_W + LRU_W
LN_EPS = 1e-5
DEPTH = 2
DEEPNORM_ALPHA = (2.0 * DEPTH) ** 0.25
NEG_INF = -1e30
ATT_SCALE = HEAD_DIM ** -0.5

LANES = 128
SUBLANES = 8
T_CHUNK = 256
CONV_HALO = 32
LRU_HALO = SUBLANES
HALF = WINDOW // 2
KEYS = WINDOW + HALF
KPAD = 256
SLAB_SKEW = 8
ROW_BLOCK = 32
PROJ_CHUNK = 512
HEADS_PER_SCORE_TILE = 2
VMEM_LIMIT_MAIN = 60 * 1024 * 1024
VMEM_LIMIT_META = 48 * 1024 * 1024


def _silu(x):
    return x * jax.nn.sigmoid(x)


def _layer_norm(x, g, b):
    mu = jnp.mean(x, axis=-1, keepdims=True)
    xc = x - mu
    var = jnp.mean(xc * xc, axis=-1, keepdims=True)
    return xc * lax.rsqrt(var + LN_EPS) * g + b


def _rotary(x, cos, s1, s2):
    return x * cos + pltpu.roll(x, LANES - ROT_DIM // 2, 1) * s1 + pltpu.roll(x, ROT_DIM // 2, 1) * s2


def _conv_rowblock(r, rb, pc_ref, cbuf_ref, dw_w_ref, dw_b_ref, g_ref, b_ref, cy_ref):
    slabs = range(CONV_W // LANES)
    accs = []
    for l in slabs:
        lo, hi = l * LANES, (l + 1) * LANES
        cbuf_ref[l, CONV_HALO + r:CONV_HALO + r + rb, :] = (
            pc_ref[r:r + rb, lo:hi] * jax.nn.sigmoid(pc_ref[r:r + rb, CONV_W + lo:CONV_W + hi]))
        accs.append(jnp.broadcast_to(dw_b_ref[:, lo:hi], (rb, LANES)))
    for k in range(CONV_K):
        start = CONV_HALO - (CONV_K - 1) + k + r
        for l in slabs:
            accs[l] = accs[l] + dw_w_ref[k:k + 1, l * LANES:(l + 1) * LANES] * cbuf_ref[l, start:start + rb, :]
    y = _layer_norm(jnp.concatenate(accs, axis=1), g_ref[...], b_ref[...])
    cy_ref[r:r + rb, :] = _silu(y).astype(BF16)


def _conv_finish(rows, rb, pc_ref, cy_ref, pw_w_ref, pw_b_ref, ycat_ref):
    z = jnp.dot(cy_ref[...], pw_w_ref[...], preferred_element_type=F32) + pw_b_ref[...]
    for r in range(0, rows, rb):
        ycat_ref[r:r + rb, 0:CONV_W] = (z[r:r + rb] * _silu(pc_ref[r:r + rb, 2 * CONV_W:3 * CONV_W])).astype(BF16)


def _lru_conv_gates(rows, rb, pr_ref, rbuf_ref, xc_ref, gb_ref, lcw_ref, lcb_ref, wg_ref, bg_ref):
    for l in range(LRU_W // LANES):
        lo, hi = l * LANES, (l + 1) * LANES
        rbuf_ref[l, LRU_HALO:LRU_HALO + rows, :] = pr_ref[:, lo:hi]
        for r in range(0, rows, rb):
            acc = jnp.broadcast_to(lcb_ref[:, lo:hi], (rb, LANES))
            for k in range(LRU_CONV_K):
                start = LRU_HALO - (LRU_CONV_K - 1) + k + r
                acc = acc + lcw_ref[k:k + 1, lo:hi] * rbuf_ref[l, start:start + rb, :]
            xc_ref[r:r + rb, lo:hi] = acc
    gb_ref[...] = jnp.dot(xc_ref[...].astype(BF16), wg_ref[...], preferred_element_type=F32) + bg_ref[...]


def _lru_group(r, carry, neg_c_softplus, pr_ref, xc_ref, gb_ref, ycat_ref, hout_ref=None):
    sub = lax.broadcasted_iota(jnp.int32, (SUBLANES, LRU_W), 0)
    x = xc_ref[r:r + SUBLANES, :]
    gate_r = jax.nn.sigmoid(gb_ref[r:r + SUBLANES, 0:LRU_W])
    gate_i = jax.nn.sigmoid(gb_ref[r:r + SUBLANES, LRU_W:2 * LRU_W])
    a = jnp.exp(gate_r * neg_c_softplus)
    u = jnp.sqrt(1.0 - a * a) * (gate_i * x)
    for d in (1, 2, 4):
        keep = sub >= d
        a_sh = jnp.where(keep, pltpu.roll(a, d, 0), 1.0)
        u_sh = jnp.where(keep, pltpu.roll(u, d, 0), 0.0)
        u = a * u_sh + u
        a = a * a_sh
    h = a * carry + u
    if hout_ref is not None:
        hout_ref[r:r + SUBLANES, :] = h
    ycat_ref[r:r + SUBLANES, CONV_W + ATT_W:OUT_IN] = (
        h * _silu(pr_ref[r:r + SUBLANES, LRU_W:2 * LRU_W])).astype(BF16)
    return jnp.broadcast_to(h[SUBLANES - 1:SUBLANES, :], (SUBLANES, LRU_W))


def _deepnorm_rows(resid, mixed, g_ref, b_ref):
    return _layer_norm(DEEPNORM_ALPHA * resid + mixed, g_ref[...], b_ref[...])


def _main_kernel(layer,
                 x_ref, cos_ref, s1_ref, s2_ref, lng_ref, lnb_ref, w_in_ref,
                 dw_w_ref, dw_b_ref, cg_ref, cb_ref, pw_w_ref, pw_b_ref, sink_ref,
                 lcw_ref, lcb_ref, wg_ref, bg_ref, lam_ref, w_out_ref, pg_ref, pb_ref,
                 cmeta_ref, kmeta_ref, vmeta_ref, rxmeta_ref, hlmeta_ref,
                 out_ref,
                 hb_ref, hres_ref, pc_ref, pa_ref, pr_ref, mix_ref, ycat_ref, cy_ref,
                 qbuf_ref, kbuf_ref, vbuf_ref, kmz_ref, vmz_ref, cbuf_ref, rbuf_ref, hcar_ref):
    first_layer = layer == 0
    T = T_CHUNK
    RB = ROW_BLOCK
    t = pl.program_id(1)

    @pl.when(t == 0)
    def _start_of_sequence():
        for l in range(CONV_W // LANES):
            cbuf_ref[l, 0:CONV_HALO - N_META, :] = jnp.zeros((CONV_HALO - N_META, LANES), F32)
            cbuf_ref[l, CONV_HALO - N_META:CONV_HALO, :] = cmeta_ref[:, l * LANES:(l + 1) * LANES]
            rbuf_ref[l, 0:LRU_HALO, :] = rxmeta_ref[N_META - LRU_HALO:N_META, l * LANES:(l + 1) * LANES]
        hcar_ref[...] = jnp.broadcast_to(hlmeta_ref[N_META - 1:N_META, :], (SUBLANES, LRU_W))
        kbuf_ref[:, 0:WINDOW, :] = jnp.zeros((N_KV, WINDOW, HEAD_DIM), BF16)
        vbuf_ref[:, 0:WINDOW, :] = jnp.zeros((N_KV, WINDOW, HEAD_DIM), BF16)
        pad = jnp.zeros((KPAD - KEYS - N_META, HEAD_DIM), BF16)
        for g in range(N_KV):
            kmz_ref[g, 0:N_META, :] = kmeta_ref[:, g * HEAD_DIM:(g + 1) * HEAD_DIM].astype(BF16)
            kmz_ref[g, N_META:KPAD - KEYS, :] = pad
            vmz_ref[g, 0:N_META, :] = vmeta_ref[:, g * HEAD_DIM:(g + 1) * HEAD_DIM].astype(BF16)
            vmz_ref[g, N_META:KPAD - KEYS, :] = pad

    def proj_chunk(dst_ref, c0, w0, n):
        dst_ref[:, c0:c0 + n] = jnp.dot(hb_ref[...], w_in_ref[:, w0:w0 + n], preferred_element_type=F32)

    resid_ref = hres_ref if first_layer else x_ref.at[0]
    for r in range(0, T, RB):
        h = x_ref[0, r:r + RB, :]
        if first_layer:
            h = _layer_norm(h, lng_ref[...], lnb_ref[...])
            hres_ref[r:r + RB, :] = h
        hb_ref[r:r + RB, :] = h.astype(BF16)
    proj_chunk(pc_ref, 0, 0, COL_ATT)

    for r in range(0, T, RB):
        _conv_rowblock(r, RB, pc_ref, cbuf_ref, dw_w_ref, dw_b_ref, cg_ref, cb_ref, cy_ref)
    for l in range(CONV_W // LANES):
        cbuf_ref[l, 0:CONV_HALO, :] = cbuf_ref[l, T:T + CONV_HALO, :]
    for c in range(0, COL_LRU - COL_ATT, PROJ_CHUNK):
        proj_chunk(pa_ref, c, COL_ATT + c, PROJ_CHUNK)
    for c in range(0, IN_TOTAL - COL_LRU, PROJ_CHUNK):
        proj_chunk(pr_ref, c, COL_LRU + c, PROJ_CHUNK)
    _conv_finish(T, RB, pc_ref, cy_ref, pw_w_ref, pw_b_ref, ycat_ref)
    mix_ref[...] = jnp.dot(ycat_ref[:, 0:CONV_W], w_out_ref[0:CONV_W, :], preferred_element_type=F32)

    cos, s1, s2 = cos_ref[...], s1_ref[...], s2_ref[...]
    for j in range(ATT_W // LANES):
        rot = _rotary(pa_ref[:, j * LANES:(j + 1) * LANES], cos, s1, s2) * ATT_SCALE
        qbuf_ref[2 * j, :, :] = rot[:, 0:HEAD_DIM].astype(BF16)
        qbuf_ref[2 * j + 1, :, :] = rot[:, HEAD_DIM:LANES].astype(BF16)
    for j in range(KV_W // LANES):
        rot = _rotary(pa_ref[:, ATT_W + j * LANES:ATT_W + (j + 1) * LANES], cos, s1, s2)
        kbuf_ref[2 * j, WINDOW:WINDOW + T, :] = rot[:, 0:HEAD_DIM].astype(BF16)
        kbuf_ref[2 * j + 1, WINDOW:WINDOW + T, :] = rot[:, HEAD_DIM:LANES].astype(BF16)
        v = pa_ref[:, ATT_W + KV_W + j * LANES:ATT_W + KV_W + (j + 1) * LANES]
        vbuf_ref[2 * j, WINDOW:WINDOW + T, :] = v[:, 0:HEAD_DIM].astype(BF16)
        vbuf_ref[2 * j + 1, WINDOW:WINDOW + T, :] = v[:, HEAD_DIM:LANES].astype(BF16)

    gb_ref = pc_ref.at[:, 0:2 * LRU_W]
    xc_ref = pc_ref.at[:, 2 * LRU_W:2 * LRU_W + LRU_W]
    _lru_conv_gates(T, RB, pr_ref, rbuf_ref, xc_ref, gb_ref, lcw_ref, lcb_ref, wg_ref, bg_ref)
    for l in range(LRU_W // LANES):
        rbuf_ref[l, 0:LRU_HALO, :] = rbuf_ref[l, T:T + LRU_HALO, :]

    hp = HEADS_PER_SCORE_TILE
    nrow = hp * HALF
    qi = lax.broadcasted_iota(jnp.int32, (nrow, KPAD), 0) % HALF
    kc = lax.broadcasted_iota(jnp.int32, (nrow, KPAD), 1)
    visible = ((kc > qi) & (kc <= qi + WINDOW)) | ((kc >= KEYS) & (kc < KEYS + N_META))
    rowp = lax.broadcasted_iota(jnp.int32, (nrow, 1), 0) // HALF
    a_gate0 = ATT_W + 2 * KV_W

    def attn_sub_block(g, blk, s):
        r0 = blk * WINDOW + s * HALF
        kw = jnp.concatenate([kbuf_ref[g, r0:r0 + KEYS, :], kmz_ref[g]], axis=0)
        vw = jnp.concatenate([vbuf_ref[g, r0:r0 + KEYS, :], vmz_ref[g]], axis=0)
        for h0 in range(g * GROUP, (g + 1) * GROUP, hp):
            sink_col = jnp.zeros((nrow, 1), F32)
            for p in range(hp):
                sink_col = jnp.where(rowp == p, sink_ref[layer, h0 + p], sink_col)
            q = jnp.concatenate([qbuf_ref[h0 + p, r0:r0 + HALF, :] for p in range(hp)], axis=0)
            sc = lax.dot_general(q, kw, (((1,), (1,)), ((), ())), preferred_element_type=F32)
            if blk == 0:
                first_key = jnp.where(t == 0, WINDOW - s * HALF, 0)
                sc = jnp.where(visible & (kc >= first_key), sc, NEG_INF)
            else:
                sc = jnp.where(visible, sc, NEG_INF)
            m = jnp.maximum(jnp.max(sc, axis=-1, keepdims=True), sink_col)
            pexp = jnp.exp(sc - m)
            den = jnp.sum(pexp, axis=-1, keepdims=True) + jnp.exp(sink_col - m)
            o = jnp.dot(pexp.astype(BF16), vw, preferred_element_type=F32) / den
            for p in range(hp):
                hcol = (h0 + p) * HEAD_DIM
                gate = _silu(pa_ref[r0:r0 + HALF, a_gate0 + hcol:a_gate0 + hcol + HEAD_DIM])
                ycat_ref[r0:r0 + HALF, CONV_W + hcol:CONV_W + hcol + HEAD_DIM] = (
                    o[p * HALF:(p + 1) * HALF, :] * gate).astype(BF16)

    sub_blocks = [(g, blk, s) for g in range(N_KV) for blk in range(T // WINDOW) for s in range(2)]
    groups_per_sub_block = (T // SUBLANES) // len(sub_blocks)
    neg_c_softplus = -LRU_C * jax.nn.softplus(-lam_ref[...])
    carry = hcar_ref[...]
    r_lru = 0
    for g, blk, s in sub_blocks:
        for _ in range(groups_per_sub_block):
            if r_lru < T:
                carry = _lru_group(r_lru, carry, neg_c_softplus, pr_ref, xc_ref, gb_ref, ycat_ref)
                r_lru += SUBLANES
        if r_lru == T:
            hcar_ref[...] = carry
            mix_ref[...] += jnp.dot(ycat_ref[:, CONV_W + ATT_W:OUT_IN], w_out_ref[CONV_W + ATT_W:OUT_IN, :],
                                    preferred_element_type=F32)
            r_lru += SUBLANES
        attn_sub_block(g, blk, s)
    assert r_lru == T + SUBLANES
    kbuf_ref[:, 0:WINDOW, :] = kbuf_ref[:, T:T + WINDOW, :]
    vbuf_ref[:, 0:WINDOW, :] = vbuf_ref[:, T:T + WINDOW, :]

    mix_ref[...] += jnp.dot(ycat_ref[:, CONV_W:CONV_W + ATT_W], w_out_ref[CONV_W:CONV_W + ATT_W, :],
                            preferred_element_type=F32)
    for r in range(0, T, RB):
        out_ref[0, r:r + RB, :] = _deepnorm_rows(resid_ref[r:r + RB, :], mix_ref[r:r + RB, :], pg_ref, pb_ref)


def _meta_kernel(layer,
                 x_ref, cos_ref, s1_ref, s2_ref, lng_ref, lnb_ref, w_in_ref,
                 dw_w_ref, dw_b_ref, cg_ref, cb_ref, pw_w_ref, pw_b_ref, sink_ref,
                 lcw_ref, lcb_ref, wg_ref, bg_ref, lam_ref, w_out_ref, pg_ref, pb_ref,
                 out_ref, cmeta_ref, kmeta_ref, vmeta_ref, rxmeta_ref, hlmeta_ref,
                 hb_ref, hres_ref, pc_ref, pa_ref, pr_ref, gb_ref, mixed_ref, ycat_ref, cy_ref,
                 cbuf_ref, rbuf_ref, xc_ref):
    M = N_META
    if layer == 0:
        hres_ref[...] = _layer_norm(x_ref[...], lng_ref[...], lnb_ref[...])
    else:
        hres_ref[...] = x_ref[...]
    hb_ref[...] = hres_ref[...].astype(BF16)

    pc_ref[...] = jnp.dot(hb_ref[...], w_in_ref[:, 0:COL_ATT], preferred_element_type=F32)
    for l in range(CONV_W // LANES):
        cbuf_ref[l, 0:CONV_HALO, :] = jnp.zeros((CONV_HALO, LANES), F32)
    _conv_rowblock(0, M, pc_ref, cbuf_ref, dw_w_ref, dw_b_ref, cg_ref, cb_ref, cy_ref)
    _conv_finish(M, M, pc_ref, cy_ref, pw_w_ref, pw_b_ref, ycat_ref)
    for l in range(CONV_W // LANES):
        cmeta_ref[:, l * LANES:(l + 1) * LANES] = cbuf_ref[l, CONV_HALO:CONV_HALO + M, :]

    pa_ref[...] = jnp.dot(hb_ref[...], w_in_ref[:, COL_ATT:COL_LRU], preferred_element_type=F32)
    cos, s1, s2 = cos_ref[...], s1_ref[...], s2_ref[...]
    qs = [_rotary(pa_ref[:, j * LANES:(j + 1) * LANES], cos, s1, s2) * ATT_SCALE for j in range(ATT_W // LANES)]
    for j in range(KV_W // LANES):
        kmeta_ref[:, j * LANES:(j + 1) * LANES] = _rotary(
            pa_ref[:, ATT_W + j * LANES:ATT_W + (j + 1) * LANES], cos, s1, s2)
    vmeta_ref[...] = pa_ref[:, ATT_W + KV_W:ATT_W + 2 * KV_W]
    nrow = GROUP * M
    qi = lax.broadcasted_iota(jnp.int32, (nrow, M), 0) % M
    kc = lax.broadcasted_iota(jnp.int32, (nrow, M), 1)
    rowp = lax.broadcasted_iota(jnp.int32, (nrow, 1), 0) // M
    a_gate0 = ATT_W + 2 * KV_W
    for g in range(N_KV):
        sink_col = jnp.zeros((nrow, 1), F32)
        for p in range(GROUP):
            sink_col = jnp.where(rowp == p, sink_ref[layer, g * GROUP + p], sink_col)
        heads = []
        for p in range(GROUP):
            hd = g * GROUP + p
            slab = qs[hd // 2]
            heads.append(slab[:, (hd % 2) * HEAD_DIM:(hd % 2 + 1) * HEAD_DIM].astype(BF16))
        q = jnp.concatenate(heads, axis=0)
        k = kmeta_ref[:, g * HEAD_DIM:(g + 1) * HEAD_DIM].astype(BF16)
        v = vmeta_ref[:, g * HEAD_DIM:(g + 1) * HEAD_DIM].astype(BF16)
        sc = lax.dot_general(q, k, (((1,), (1,)), ((), ())), preferred_element_type=F32)
        sc = jnp.where(kc <= qi, sc, NEG_INF)
        m = jnp.maximum(jnp.max(sc, axis=-1, keepdims=True), sink_col)
        pexp = jnp.exp(sc - m)
        den = jnp.sum(pexp, axis=-1, keepdims=True) + jnp.exp(sink_col - m)
        o = jnp.dot(pexp.astype(BF16), v, preferred_element_type=F32) / den
        for p in range(GROUP):
            hcol = (g * GROUP + p) * HEAD_DIM
            gate = _silu(pa_ref[:, a_gate0 + hcol:a_gate0 + hcol + HEAD_DIM])
            ycat_ref[:, CONV_W + hcol:CONV_W + hcol + HEAD_DIM] = (o[p * M:(p + 1) * M, :] * gate).astype(BF16)

    pr_ref[...] = jnp.dot(hb_ref[...], w_in_ref[:, COL_LRU:IN_TOTAL], preferred_element_type=F32)
    for l in range(LRU_W // LANES):
        rbuf_ref[l, 0:LRU_HALO, :] = jnp.zeros((LRU_HALO, LANES), F32)
    rxmeta_ref[...] = pr_ref[:, 0:LRU_W]
    _lru_conv_gates(M, M, pr_ref, rbuf_ref, xc_ref, gb_ref, lcw_ref, lcb_ref, wg_ref, bg_ref)
    neg_c_softplus = -LRU_C * jax.nn.softplus(-lam_ref[...])
    carry = jnp.zeros((SUBLANES, LRU_W), F32)
    for r in range(0, M, SUBLANES):
        carry = _lru_group(r, carry, neg_c_softplus, pr_ref, xc_ref, gb_ref, ycat_ref, hout_ref=hlmeta_ref)

    mixed_ref[...] = jnp.dot(ycat_ref[...], w_out_ref[...], preferred_element_type=F32)
    out_ref[...] = _deepnorm_rows(hres_ref[...], mixed_ref[...], pg_ref, pb_ref)


def _rope_tables(pos):
    half = ROT_DIM // 2
    inv_freq = ROPE_THETA ** (-jnp.arange(half, dtype=F32) / half)
    ang = pos.astype(F32)[:, None] * inv_freq[None, :]
    cos, sin = jnp.cos(ang), jnp.sin(ang)
    n = pos.shape[0]
    one = jnp.ones((n, HEAD_DIM - ROT_DIM), F32)
    zero = jnp.zeros((n, HEAD_DIM - ROT_DIM), F32)
    zh = jnp.zeros((n, half), F32)
    c64 = jnp.concatenate([cos, cos, one], axis=1)
    s1_64 = jnp.concatenate([-sin, zh, zero], axis=1)
    s2_64 = jnp.concatenate([zh, sin, zero], axis=1)
    rep = LANES // HEAD_DIM
    return jnp.tile(c64, (1, rep)), jnp.tile(s1_64, (1, rep)), jnp.tile(s2_64, (1, rep))


def _block_diag(w):
    h, d, _ = w.shape
    eye = jnp.eye(h, dtype=w.dtype)
    return (eye[:, None, :, None] * w[:, :, None, :]).reshape(h * d, h * d)


def _resident(shape, layer=None):
    nd = len(shape)
    if layer is None:
        return pl.BlockSpec(shape, lambda *_: (0,) * nd, pipeline_mode=pl.Buffered(1))
    return pl.BlockSpec((None,) + shape, lambda *_: (layer,) + (0,) * nd, pipeline_mode=pl.Buffered(1))


def _stacked_params(ln_in_g, ln_in_b, w_in, conv_dw_w, conv_dw_b, conv_ln_g, conv_ln_b, conv_pw_w, conv_pw_b,
                    attn_sinks, lru_conv_w, lru_conv_b, lru_wa, lru_ba, lru_wx, lru_bx, lru_lambda, w_out,
                    ln_post_g, ln_post_b):
    rows = lambda a: a.reshape(a.shape[0], 1, -1)
    w_gate = jnp.concatenate([jax.vmap(_block_diag)(lru_wa), jax.vmap(_block_diag)(lru_wx)], axis=-1).astype(BF16)
    b_gate = jnp.concatenate([lru_ba, lru_bx], axis=-1)
    return [ln_in_g.reshape(1, -1), ln_in_b.reshape(1, -1), w_in.astype(BF16),
            conv_dw_w, rows(conv_dw_b), rows(conv_ln_g), rows(conv_ln_b), conv_pw_w.astype(BF16), rows(conv_pw_b),
            attn_sinks,
            lru_conv_w, rows(lru_conv_b), w_gate, rows(b_gate), rows(lru_lambda),
            w_out.astype(BF16), rows(ln_post_g), rows(ln_post_b)]


def _param_specs(l):
    return [_resident((1, D_MODEL)), _resident((1, D_MODEL)), _resident((D_MODEL, IN_TOTAL), l),
            _resident((CONV_K, CONV_W), l), _resident((1, CONV_W), l), _resident((1, CONV_W), l),
            _resident((1, CONV_W), l), _resident((CONV_W, CONV_W), l), _resident((1, CONV_W), l),
            pl.BlockSpec(memory_space=pltpu.SMEM),
            _resident((LRU_CONV_K, LRU_W), l), _resident((1, LRU_W), l), _resident((LRU_W, 2 * LRU_W), l),
            _resident((1, 2 * LRU_W), l), _resident((1, LRU_W), l),
            _resident((OUT_IN, D_MODEL), l), _resident((1, D_MODEL), l), _resident((1, D_MODEL), l)]


def _meta_call(layer, h_meta, tables, params):
    M = N_META
    f = lambda *shape: jax.ShapeDtypeStruct(shape, F32)
    out_shape = [f(M, D_MODEL), f(M, CONV_W), f(M, KV_W), f(M, KV_W), f(M, LRU_W), f(M, LRU_W)]
    in_specs = [_resident((M, D_MODEL))] + [_resident((M, LANES))] * 3 + _param_specs(layer)
    out_specs = [pl.BlockSpec(s.shape, lambda i: (0, 0)) for s in out_shape]
    scratch = [pltpu.VMEM((M, D_MODEL), BF16), pltpu.VMEM((M, D_MODEL), F32),
               pltpu.VMEM((M, COL_ATT), F32), pltpu.VMEM((M, COL_LRU - COL_ATT), F32),
               pltpu.VMEM((M, 2 * LRU_W), F32), pltpu.VMEM((M, 2 * LRU_W), F32),
               pltpu.VMEM((M, D_MODEL), F32), pltpu.VMEM((M, OUT_IN), BF16), pltpu.VMEM((M, CONV_W), BF16),
               pltpu.VMEM((CONV_W // LANES, CONV_HALO + M, LANES), F32),
               pltpu.VMEM((LRU_W // LANES, LRU_HALO + M, LANES), F32),
               pltpu.VMEM((M, LRU_W), F32)]
    return pl.pallas_call(
        functools.partial(_meta_kernel, layer),
        grid=(1,), in_specs=in_specs, out_specs=out_specs, out_shape=out_shape, scratch_shapes=scratch,
        compiler_params=pltpu.CompilerParams(dimension_semantics=("arbitrary",), vmem_limit_bytes=VMEM_LIMIT_META),
        name=f"meta_layer{layer + 1}",
    )(h_meta, *tables, *params)


def _main_call(layer, h, tables, params, meta_state):
    B, S, _ = h.shape
    T = T_CHUNK
    M = N_META
    tok = pl.BlockSpec((1, T, D_MODEL), lambda b, t: (b, t, 0))
    tab = pl.BlockSpec((T, LANES), lambda b, t: (t, 0))
    in_specs = ([tok, tab, tab, tab] + _param_specs(layer)
                + [_resident((M, CONV_W)), _resident((M, KV_W)), _resident((M, KV_W)),
                   _resident((M, LRU_W)), _resident((M, LRU_W))])
    scratch = [pltpu.VMEM((T, D_MODEL), BF16), pltpu.VMEM((T, D_MODEL) if layer == 0 else (SUBLANES, LANES), F32),
               pltpu.VMEM((T, COL_ATT), F32), pltpu.VMEM((T, COL_LRU - COL_ATT), F32),
               pltpu.VMEM((T, 2 * LRU_W), F32),
               pltpu.VMEM((T, D_MODEL), F32), pltpu.VMEM((T, OUT_IN), BF16), pltpu.VMEM((T, CONV_W), BF16),
               pltpu.VMEM((N_HEADS, T, HEAD_DIM), BF16),
               pltpu.VMEM((N_KV, WINDOW + T, HEAD_DIM), BF16), pltpu.VMEM((N_KV, WINDOW + T, HEAD_DIM), BF16),
               pltpu.VMEM((N_KV, KPAD - KEYS, HEAD_DIM), BF16), pltpu.VMEM((N_KV, KPAD - KEYS, HEAD_DIM), BF16),
               pltpu.VMEM((CONV_W // LANES, CONV_HALO + T + SLAB_SKEW, LANES), F32),
               pltpu.VMEM((LRU_W // LANES, LRU_HALO + T + SLAB_SKEW, LANES), F32),
               pltpu.VMEM((SUBLANES, LRU_W), F32)]
    return pl.pallas_call(
        functools.partial(_main_kernel, layer),
        grid=(B, S // T), in_specs=in_specs, out_specs=tok,
        out_shape=jax.ShapeDtypeStruct((B, S, D_MODEL), F32), scratch_shapes=scratch,
        compiler_params=pltpu.CompilerParams(dimension_semantics=("arbitrary", "arbitrary"),
                                             vmem_limit_bytes=VMEM_LIMIT_MAIN),
        name=f"tokens_layer{layer + 1}",
    )(h, *tables, *params, *meta_state)


def kernel(x, meta_tokens, ln_in_g, ln_in_b, w_in, conv_dw_w, conv_dw_b, conv_ln_g, conv_ln_b, conv_pw_w, conv_pw_b,
           attn_sinks, lru_conv_w, lru_conv_b, lru_wa, lru_ba, lru_wx, lru_bx, lru_lambda, w_out, ln_post_g,
           ln_post_b):
    B, S, D = x.shape
    assert D == D_MODEL and S % T_CHUNK == 0 and w_in.shape == (DEPTH, D_MODEL, IN_TOTAL)
    meta_tables = _rope_tables(jnp.arange(N_META, dtype=jnp.int32))
    tok_tables = _rope_tables(N_META + jnp.arange(S, dtype=jnp.int32))
    params = _stacked_params(ln_in_g, ln_in_b, w_in, conv_dw_w, conv_dw_b, conv_ln_g, conv_ln_b, conv_pw_w, conv_pw_b,
                             attn_sinks, lru_conv_w, lru_conv_b, lru_wa, lru_ba, lru_wx, lru_bx, lru_lambda, w_out,
                             ln_post_g, ln_post_b)
    h, h_meta = x, meta_tokens.astype(x.dtype)
    for l in range(DEPTH):
        h_meta, *meta_state = _meta_call(l, h_meta, meta_tables, params)
        h = _main_call(l, h, tok_tables, params, meta_state)
    return h
```
